```python
import math
import jax, jax.numpy as jnp
from jax import lax
import numpy as np

D_MODEL = 2048
BATCH = 4
SEQ = 2048
DEPTH = 4
DEC_BATCH = 8
DEC_SEQ = 1
PAST_LEN = 16384
PAGE_SIZE = 128

F32 = jnp.float32
ROPE_THETA = 10000.0
LN_EPS = 1e-5
Q_BLOCK = 128
MIXER_KINDS = ('diff', 'dsa', 'pool')
A_HEADS = 8
A_HEAD_DIM = D_MODEL // A_HEADS // 2
B_HEADS = 16
B_HEAD_DIM = D_MODEL // B_HEADS
B_KV_HEADS = 4
B_GROUP = B_HEADS // B_KV_HEADS
IDX_HEADS = 16
IDX_DIM = 64
IDX_TOPK_MAX = 256
DSA_SPLITS = [B_HEADS * B_HEAD_DIM,
              B_HEADS * B_HEAD_DIM + B_KV_HEADS * B_HEAD_DIM,
              B_HEADS * B_HEAD_DIM + 2 * B_KV_HEADS * B_HEAD_DIM,
              B_HEADS * B_HEAD_DIM + 2 * B_KV_HEADS * B_HEAD_DIM + IDX_HEADS * IDX_DIM,
              B_HEADS * B_HEAD_DIM + 2 * B_KV_HEADS * B_HEAD_DIM + IDX_HEADS * IDX_DIM + IDX_DIM]
DSA_IN_DIM = DSA_SPLITS[-1] + IDX_HEADS
POOL_WINDOWS = (2, 4, 8, 16)
POOL_GROUPS = 4
POOL_GROUP_DIM = D_MODEL // POOL_GROUPS
POOL_STATE_LEN = max(POOL_WINDOWS) - 1
N_EXPERTS = 16
N_EXPERT_GROUPS = 4
EXPERTS_PER_GROUP = N_EXPERTS // N_EXPERT_GROUPS
TOPK_EXPERTS = 2
D_EXPERT = D_MODEL // 4
DEEPNORM_ALPHA = (2 * DEPTH) ** 0.25
DEEPNORM_BETA = (8 * DEPTH) ** -0.25

kernel_name = 'hybrid_diffattn_dsa_pool_grouped_moe_step'


def layer_norm(x, g, b):
    xf = x.astype(F32)
    mu = jnp.mean(xf, axis=-1, keepdims=True)
    var = jnp.mean(jnp.square(xf - mu), axis=-1, keepdims=True)
    return (xf - mu) * lax.rsqrt(var + LN_EPS) * g.astype(F32) + b.astype(F32)


def post_norm(x, sub, g, b):
    return layer_norm(DEEPNORM_ALPHA * x.astype(F32) + sub.astype(F32), g, b).astype(x.dtype)


def rope(x, pos):
    dh = x.shape[-1]
    half = dh // 2
    inv_freq = ROPE_THETA ** (-jnp.arange(half, dtype=F32) * 2.0 / dh)
    ang = pos.astype(F32)[:, None] * inv_freq[None, :]
    cos = jnp.cos(ang)[None, :, None, :]
    sin = jnp.sin(ang)[None, :, None, :]
    xf = x.astype(F32)
    x1, x2 = xf[..., :half], xf[..., half:]
    return jnp.concatenate([x1 * cos - x2 * sin, x2 * cos + x1 * sin], axis=-1).astype(x.dtype)


def gather_pages(pool, page_table):
    g = pool[page_table]
    return g.reshape((g.shape[0], g.shape[1] * g.shape[2]) + g.shape[3:])


def gather_selected(pool, new_rows, page_table, idx):
    past_pos = jnp.minimum(idx, PAST_LEN - 1)
    phys = jax.vmap(lambda pt, p: pt[p])(page_table, past_pos // PAGE_SIZE)
    from_past = pool[phys, past_pos % PAGE_SIZE]
    new_pos = jnp.clip(idx - PAST_LEN, 0, new_rows.shape[1] - 1)
    from_new = jax.vmap(lambda rows, p: rows[p])(new_rows, new_pos)
    is_new = (idx >= PAST_LEN).reshape(idx.shape + (1,) * (from_past.ndim - idx.ndim))
    return jnp.where(is_new, from_new, from_past)


def map_query_blocks(fn, *q_args):
    n_q = q_args[0].shape[1]
    nb = n_q // Q_BLOCK
    def split(a):
        return jnp.moveaxis(a.reshape((a.shape[0], nb, Q_BLOCK) + a.shape[2:]), 1, 0)
    out = lax.map(lambda blk: fn(*blk), tuple(split(a) for a in q_args))
    out = jnp.moveaxis(out, 0, 1)
    return out.reshape((out.shape[0], n_q) + out.shape[3:])


def diff_qkv(x, pos, w_qkv):
    b, t, _ = x.shape
    dqk = 2 * A_HEADS * A_HEAD_DIM
    q, k, v = jnp.split(x @ w_qkv, [dqk, 2 * dqk], axis=-1)
    q = rope(q.reshape(b, t, 2 * A_HEADS, A_HEAD_DIM), pos)
    k = rope(k.reshape(b, t, 2 * A_HEADS, A_HEAD_DIM), pos)
    v = v.reshape(b, t, A_HEADS, 2 * A_HEAD_DIM)
    return q, k, v


def diff_attn_core(q, qpos, k, v, kpos, lam):
    b, tq = q.shape[:2]
    tk = k.shape[1]
    s = jnp.einsum('bqhd,bkhd->bhqk', q, k, preferred_element_type=F32) * (A_HEAD_DIM ** -0.5)
    causal = kpos[:, None, None, :] <= qpos[:, None, :, None]
    p = jax.nn.softmax(jnp.where(causal, s, -jnp.inf), axis=-1)
    p = p.reshape(b, A_HEADS, 2, tq, tk)
    p_diff = p[:, :, 0] - lam * p[:, :, 1]
    return jnp.einsum('bhqk,bkhe->bqhe', p_diff.astype(v.dtype), v)


def diff_output(o, subln_g, lam_init, w_o):
    b, t = o.shape[:2]
    of = o.astype(F32)
    of = of * lax.rsqrt(jnp.mean(jnp.square(of), axis=-1, keepdims=True) + LN_EPS) * subln_g.astype(F32) * (1.0 - lam_init)
    return of.reshape(b, t, A_HEADS * 2 * A_HEAD_DIM).astype(o.dtype) @ w_o


def diff_layer(xp, xs, cache_k, cache_v, page_table, w_qkv, w_o, lq1, lk1, lq2, lk2, subln_g, layer_idx):
    lam_init = 0.8 - 0.6 * math.exp(-0.3 * layer_idx)
    lam = (jnp.exp(jnp.sum(lq1.astype(F32) * lk1.astype(F32)))
           - jnp.exp(jnp.sum(lq2.astype(F32) * lk2.astype(F32))) + lam_init)
    b, t, _ = xp.shape
    pos_p = jnp.arange(t, dtype=jnp.int32)
    qp, kp, vp = diff_qkv(xp, pos_p, w_qkv)
    kpos_p = jnp.broadcast_to(pos_p, (b, t))
    op = map_query_blocks(lambda qb, qposb: diff_attn_core(qb, qposb, kp, vp, kpos_p, lam), qp, kpos_p)
    yp = diff_output(op, subln_g, lam_init, w_o)
    bs, ts, _ = xs.shape
    n_keys = PAST_LEN + ts
    pos_s = PAST_LEN + jnp.arange(ts, dtype=jnp.int32)
    qs, ks, vs = diff_qkv(xs, pos_s, w_qkv)
    k_all = jnp.concatenate([gather_pages(cache_k, page_table), ks], axis=1)
    v_all = jnp.concatenate([gather_pages(cache_v, page_table), vs], axis=1)
    qpos_s = jnp.broadcast_to(pos_s, (bs, ts))
    kpos_s = jnp.broadcast_to(jnp.arange(n_keys, dtype=jnp.int32), (bs, n_keys))
    os_ = diff_attn_core(qs, qpos_s, k_all, v_all, kpos_s, lam)
    ys = diff_output(os_, subln_g, lam_init, w_o)
    return yp, ys, (kp, vp, ks, vs)


def dsa_project(x, pos, w_in):
    b, t, _ = x.shape
    q, k, v, qi, ki, wi = jnp.split(x @ w_in, DSA_SPLITS, axis=-1)
    q = rope(q.reshape(b, t, B_HEADS, B_HEAD_DIM), pos)
    k = rope(k.reshape(b, t, B_KV_HEADS, B_HEAD_DIM), pos)
    v = v.reshape(b, t, B_KV_HEADS, B_HEAD_DIM)
    qi = rope(qi.reshape(b, t, IDX_HEADS, IDX_DIM), pos)
    ki = rope(ki.reshape(b, t, 1, IDX_DIM), pos)[:, :, 0]
    return q, k, v, qi, ki, wi


def indexer_scores(qi, wi, ki, qpos, kpos):
    dots = jnp.einsum('bqhd,bkd->bqhk', qi, ki, preferred_element_type=F32)
    s = jnp.einsum('bqh,bqhk->bqk', wi.astype(F32), jax.nn.relu(dots))
    return jnp.where(kpos[:, None, :] <= qpos[:, :, None], s, -jnp.inf)


def sparse_attn(q, qpos, k_sel, v_sel, sel_pos):
    b, tq = q.shape[:2]
    qg = q.reshape(b, tq, B_KV_HEADS, B_GROUP, B_HEAD_DIM)
    s = jnp.einsum('bqgrd,bqkgd->bqgrk', qg, k_sel, preferred_element_type=F32) * (B_HEAD_DIM ** -0.5)
    valid = (sel_pos <= qpos[:, :, None])[:, :, None, None, :]
    p = jax.nn.softmax(jnp.where(valid, s, -jnp.inf), axis=-1)
    o = jnp.einsum('bqgrk,bqkgd->bqgrd', p.astype(v_sel.dtype), v_sel)
    return o.reshape(b, tq, B_HEADS * B_HEAD_DIM)


def dsa_layer(xp, xs, cache_k, cache_v, cache_kidx, page_table, w_in, w_o):
    b, t, _ = xp.shape
    pos_p = jnp.arange(t, dtype=jnp.int32)
    qp, kp, vp, qip, kip, wip = dsa_project(xp, pos_p, w_in)
    kpos_p = jnp.broadcast_to(pos_p, (b, t))
    topk_p = min(IDX_TOPK_MAX, t // 4)
    def block(qb, qib, wib, qposb):
        _, idx = lax.top_k(indexer_scores(qib, wib, kip, qposb, kpos_p), topk_p)
        k_sel = jax.vmap(lambda rows, i: rows[i])(kp, idx)
        v_sel = jax.vmap(lambda rows, i: rows[i])(vp, idx)
        return sparse_attn(qb, qposb, k_sel, v_sel, idx)
    yp = map_query_blocks(block, qp, qip, wip, kpos_p) @ w_o
    bs, ts, _ = xs.shape
    n_keys = PAST_LEN + ts
    pos_s = PAST_LEN + jnp.arange(ts, dtype=jnp.int32)
    qs, ks, vs, qis, kis, wis = dsa_project(xs, pos_s, w_in)
    ki_all = jnp.concatenate([gather_pages(cache_kidx, page_table), kis], axis=1)
    qpos_s = jnp.broadcast_to(pos_s, (bs, ts))
    kpos_s = jnp.broadcast_to(jnp.arange(n_keys, dtype=jnp.int32), (bs, n_keys))
    _, idx_s = lax.top_k(indexer_scores(qis, wis, ki_all, qpos_s, kpos_s), min(IDX_TOPK_MAX, n_keys // 4))
    k_sel = gather_selected(cache_k, ks, page_table, idx_s)
    v_sel = gather_selected(cache_v, vs, page_table, idx_s)
    ys = sparse_attn(qs, qpos_s, k_sel, v_sel, idx_s) @ w_o
    return yp, ys, (kp, vp, kip, ks, vs, kis)


def pool_mix(x_ext, n_prefix, pos0, w_group, scale):
    b, lx, d = x_ext.shape
    t = lx - n_prefix
    xf = x_ext.astype(F32)
    cs = jnp.concatenate([jnp.zeros((b, 1, d), F32), jnp.cumsum(xf, axis=1)], axis=1)
    rows = n_prefix + jnp.arange(t)
    pos = pos0 + jnp.arange(t)
    outs = []
    for g, w in enumerate(POOL_WINDOWS):
        c0, c1 = g * POOL_GROUP_DIM, (g + 1) * POOL_GROUP_DIM
        win_sum = cs[:, rows + 1, c0:c1] - cs[:, jnp.maximum(rows + 1 - w, 0), c0:c1]
        count = jnp.minimum(w, pos + 1).astype(F32)[None, :, None]
        mixed = win_sum / count - xf[:, n_prefix:, c0:c1]
        outs.append(mixed.astype(x_ext.dtype) @ w_group[g])
    return jnp.concatenate(outs, axis=-1) * scale


def pool_layer(xp, xs, state_pool, w_group, scale):
    yp = pool_mix(xp, 0, 0, w_group, scale)
    xs_ext = jnp.concatenate([state_pool.astype(xs.dtype), xs], axis=1)
    ys = pool_mix(xs_ext, state_pool.shape[1], PAST_LEN, w_group, scale)
    return yp, ys, (xp[:, -POOL_STATE_LEN:], xs_ext[:, -POOL_STATE_LEN:])


def moe(x, router_w, router_bias, w_in, w_down):
    b, t, d = x.shape
    xt = x.reshape(b * t, d)
    n = xt.shape[0]
    aff = jax.nn.softmax((xt @ router_w).astype(F32), axis=-1)
    sel = aff + router_bias.astype(F32)
    group_score = lax.top_k(sel.reshape(n, N_EXPERT_GROUPS, EXPERTS_PER_GROUP), TOPK_EXPERTS)[0].sum(-1)
    best_group = jnp.argmax(group_score, axis=-1)
    in_group = (jnp.arange(N_EXPERTS) // EXPERTS_PER_GROUP)[None, :] == best_group[:, None]
    _, top_idx = lax.top_k(jnp.where(in_group, sel, -jnp.inf), TOPK_EXPERTS)
    top_aff = jnp.take_along_axis(aff, top_idx, axis=-1)
    gates = top_aff / jnp.sum(top_aff, axis=-1, keepdims=True)
    gate_dense = jnp.sum(jax.nn.one_hot(top_idx, N_EXPERTS, dtype=F32) * gates[..., None], axis=1)
    h = jnp.einsum('nd,edf->nef', xt, w_in)
    hg, hu = jnp.split(h, 2, axis=-1)
    a = jax.nn.silu(hg) * hu * gate_dense[..., None].astype(h.dtype)
    y = jnp.einsum('nef,efd->nd', a, w_down)
    return y.reshape(b, t, d)


def setup_inputs(seed: int = 0) -> dict:
    key = jax.random.key(seed)
    keys = iter(jax.random.split(key, 128))
    def nrm(shape, scale=1.0):
        return jax.random.normal(next(keys), shape, F32) * scale
    def gain(n):
        return 1.0 + nrm((n,), 0.02)
    n_pages = PAST_LEN // PAGE_SIZE
    n_used = DEC_BATCH * n_pages
    n_pool = n_used + max(1, n_used // 4)
    a_qk = 2 * A_HEADS * A_HEAD_DIM
    a_v = A_HEADS * 2 * A_HEAD_DIM
    inp = {}
    inp['x_prompt'] = nrm((BATCH, SEQ, D_MODEL))
    inp['x_sample'] = nrm((DEC_BATCH, DEC_SEQ, D_MODEL))
    inp['cache_l0_k'] = nrm((n_pool, PAGE_SIZE, 2 * A_HEADS, A_HEAD_DIM))
    inp['cache_l0_v'] = nrm((n_pool, PAGE_SIZE, A_HEADS, 2 * A_HEAD_DIM))
    inp['cache_l1_k'] = nrm((n_pool, PAGE_SIZE, B_KV_HEADS, B_HEAD_DIM))
    inp['cache_l1_v'] = nrm((n_pool, PAGE_SIZE, B_KV_HEADS, B_HEAD_DIM))
    inp['cache_l1_kidx'] = nrm((n_pool, PAGE_SIZE, IDX_DIM))
    inp['state_l2_pool'] = nrm((DEC_BATCH, POOL_STATE_LEN, D_MODEL))
    inp['cache_l3_k'] = nrm((n_pool, PAGE_SIZE, 2 * A_HEADS, A_HEAD_DIM))
    inp['cache_l3_v'] = nrm((n_pool, PAGE_SIZE, A_HEADS, 2 * A_HEAD_DIM))
    inp['page_table'] = jax.random.permutation(next(keys), n_pool)[:n_used].reshape(DEC_BATCH, n_pages).astype(jnp.int32)
    inp['router_w'] = nrm((D_MODEL, N_EXPERTS), D_MODEL ** -0.5)
    inp['router_bias'] = nrm((N_EXPERTS,), 0.01)
    for i in range(DEPTH):
        kind = MIXER_KINDS[i % len(MIXER_KINDS)]
        p = 'l%d_' % i
        if kind == 'diff':
            inp[p + 'w_qkv'] = nrm((D_MODEL, 2 * a_qk + a_v), D_MODEL ** -0.5)
            inp[p + 'w_o'] = nrm((a_v, D_MODEL), a_v ** -0.5 * DEEPNORM_BETA)
            for nm in ('lam_q1', 'lam_k1', 'lam_q2', 'lam_k2'):
                inp[p + nm] = nrm((A_HEAD_DIM,), 0.1)
            inp[p + 'subln_g'] = gain(2 * A_HEAD_DIM)
        elif kind == 'dsa':
            inp[p + 'w_in'] = nrm((D_MODEL, DSA_IN_DIM), D_MODEL ** -0.5)
            inp[p + 'w_o'] = nrm((B_HEADS * B_HEAD_DIM, D_MODEL), (B_HEADS * B_HEAD_DIM) ** -0.5 * DEEPNORM_BETA)
        else:
            inp[p + 'w_group'] = nrm((POOL_GROUPS, POOL_GROUP_DIM, POOL_GROUP_DIM), POOL_GROUP_DIM ** -0.5 * DEEPNORM_BETA)
            inp[p + 'scale'] = gain(D_MODEL)
        inp[p + 'ln1_g'] = gain(D_MODEL)
        inp[p + 'ln1_b'] = nrm((D_MODEL,), 0.02)
        inp[p + 'moe_w_in'] = nrm((N_EXPERTS, D_MODEL, 2 * D_EXPERT), D_MODEL ** -0.5)
        inp[p + 'moe_w_down'] = nrm((N_EXPERTS, D_EXPERT, D_MODEL), D_EXPERT ** -0.5 * DEEPNORM_BETA)
        inp[p + 'ln2_g'] = gain(D_MODEL)
        inp[p + 'ln2_b'] = nrm((D_MODEL,), 0.02)
    return inp


def reference(x_prompt, x_sample, cache_l0_k, cache_l0_v, cache_l1_k, cache_l1_v, cache_l1_kidx,
              state_l2_pool, cache_l3_k, cache_l3_v, page_table, router_w, router_bias,
              l0_w_qkv, l0_w_o, l0_lam_q1, l0_lam_k1, l0_lam_q2, l0_lam_k2, l0_subln_g,
              l0_ln1_g, l0_ln1_b, l0_moe_w_in, l0_moe_w_down, l0_ln2_g, l0_ln2_b,
              l1_w_in, l1_w_o,
              l1_ln1_g, l1_ln1_b, l1_moe_w_in, l1_moe_w_down, l1_ln2_g, l1_ln2_b,
              l2_w_group, l2_scale,
              l2_ln1_g, l2_ln1_b, l2_moe_w_in, l2_moe_w_down, l2_ln2_g, l2_ln2_b,
              l3_w_qkv, l3_w_o, l3_lam_q1, l3_lam_k1, l3_lam_q2, l3_lam_k2, l3_subln_g,
              l3_ln1_g, l3_ln1_b, l3_moe_w_in, l3_moe_w_down, l3_ln2_g, l3_ln2_b):
    mixer_p = [(l0_w_qkv, l0_w_o, l0_lam_q1, l0_lam_k1, l0_lam_q2, l0_lam_k2, l0_subln_g),
               (l1_w_in, l1_w_o),
               (l2_w_group, l2_scale),
               (l3_w_qkv, l3_w_o, l3_lam_q1, l3_lam_k1, l3_lam_q2, l3_lam_k2, l3_subln_g)]
    layer_caches = [(cache_l0_k, cache_l0_v),
                    (cache_l1_k, cache_l1_v, cache_l1_kidx),
                    (state_l2_pool,),
                    (cache_l3_k, cache_l3_v)]
    block_p = [(l0_ln1_g, l0_ln1_b, l0_moe_w_in, l0_moe_w_down, l0_ln2_g, l0_ln2_b),
               (l1_ln1_g, l1_ln1_b, l1_moe_w_in, l1_moe_w_down, l1_ln2_g, l1_ln2_b),
               (l2_ln1_g, l2_ln1_b, l2_moe_w_in, l2_moe_w_down, l2_ln2_g, l2_ln2_b),
               (l3_ln1_g, l3_ln1_b, l3_moe_w_in, l3_moe_w_down, l3_ln2_g, l3_ln2_b)]
    xp, xs = x_prompt, x_sample
    new_state = []
    for i in range(DEPTH):
        kind = MIXER_KINDS[i % len(MIXER_KINDS)]
        if kind == 'diff':
            mp, ms, st = diff_layer(xp, xs, *layer_caches[i], page_table, *mixer_p[i], i)
        elif kind == 'dsa':
            mp, ms, st = dsa_layer(xp, xs, *layer_caches[i], page_table, *mixer_p[i])
        else:
            mp, ms, st = pool_layer(xp, xs, *layer_caches[i], *mixer_p[i])
        ln1_g, ln1_b, w_in, w_down, ln2_g, ln2_b = block_p[i]
        xp = post_norm(xp, mp, ln1_g, ln1_b)
        xs = post_norm(xs, ms, ln1_g, ln1_b)
        xp = post_norm(xp, moe(xp, router_w, router_bias, w_in, w_down), ln2_g, ln2_b)
        xs = post_norm(xs, moe(xs, router_w, router_bias, w_in, w_down), ln2_g, ln2_b)
        new_state.append(st)
    l0_k_p, l0_v_p, l0_k_s, l0_v_s = new_state[0]
    l1_k_p, l1_v_p, l1_kidx_p, l1_k_s, l1_v_s, l1_kidx_s = new_state[1]
    l2_pool_p, l2_pool_s = new_state[2]
    l3_k_p, l3_v_p, l3_k_s, l3_v_s = new_state[3]
    return (xp, xs, l0_k_p, l0_v_p, l0_k_s, l0_v_s,
            l1_k_p, l1_v_p, l1_kidx_p, l1_k_s, l1_v_s, l1_kidx_s,
            l2_pool_p, l2_pool_s,
            l3_k_p, l3_v_p, l3_k_s, l3_v_s)
```

```python
import functools
import math

import jax
import jax.numpy as jnp
from jax import lax
from jax.experimental import pallas as pl
from jax.experimental.pallas import tpu as pltpu

F32 = jnp.float32
BF16 = jnp.bfloat16

D_MODEL = 2048
DEPTH = 4
PAST_LEN = 16384
PAGE_SIZE = 128
ROPE_THETA = 10000.0
LN_EPS = 1e-5
A_HEADS = 8
A_HEAD_DIM = D_MODEL // A_HEADS // 2
B_HEADS = 16
B_HEAD_DIM = D_MODEL // B_HEADS
B_KV_HEADS = 4
B_GROUP = B_HEADS // B_KV_HEADS
IDX_HEADS = 16
IDX_DIM = 64
IDX_TOPK_MAX = 256
POOL_WINDOWS = (2, 4, 8, 16)
POOL_GROUPS = 4
POOL_GROUP_DIM = D_MODEL // POOL_GROUPS
POOL_STATE_LEN = max(POOL_WINDOWS) - 1
N_EXPERTS = 16
N_EXPERT_GROUPS = 4
EXPERTS_PER_GROUP = N_EXPERTS // N_EXPERT_GROUPS
D_EXPERT = D_MODEL // 4
DEEPNORM_ALPHA = (2 * DEPTH) ** 0.25

LANES = 128
SUBLANES = 8
VMEM_LIMIT_BYTES = 56 * 1024 * 1024

NEG_BIG = -1e30
HIGHEST = lax.Precision.HIGHEST


def _params(*sem):
    return pltpu.CompilerParams(dimension_semantics=sem, vmem_limit_bytes=VMEM_LIMIT_BYTES)


def _dot(a, b, precise):
    if precise:
        return jnp.dot(a.astype(F32), b.astype(F32), precision=HIGHEST, preferred_element_type=F32)
    return jnp.dot(a.astype(BF16), b.astype(BF16), preferred_element_type=F32)


def _dot_nt(a, b, precise):
    dims = (((1,), (1,)), ((), ()))
    if precise:
        return lax.dot_general(a.astype(F32), b.astype(F32), dims, precision=HIGHEST,
                               preferred_element_type=F32)
    return lax.dot_general(a.astype(BF16), b.astype(BF16), dims, preferred_element_type=F32)


def _layer_norm_rows(r, g, b):
    mu = jnp.mean(r, axis=-1, keepdims=True)
    c = r - mu
    var = jnp.mean(c * c, axis=-1, keepdims=True)
    return c * lax.rsqrt(var + LN_EPS) * g + b


def _rope_tables(pos, head_dim):
    half = head_dim // 2
    inv_freq = ROPE_THETA ** (-jnp.arange(half, dtype=F32) * 2.0 / head_dim)
    ang = pos.astype(F32)[:, None] * inv_freq[None, :]
    cos, sin = jnp.cos(ang), jnp.sin(ang)
    reps = LANES // head_dim
    cos_t = jnp.tile(jnp.concatenate([cos, cos], axis=-1), (1, reps))
    sin_t = jnp.tile(jnp.concatenate([-sin, sin], axis=-1), (1, reps))
    return cos_t, sin_t


def _rope_lanes(x, cos_t, sin_t, head_dim):
    half = head_dim // 2
    if head_dim == LANES:
        partner = pltpu.roll(x, half, axis=1)
    else:
        lane = lax.broadcasted_iota(jnp.int32, x.shape, 1)
        first = (lane % head_dim) < half
        partner = jnp.where(first, pltpu.roll(x, LANES - half, axis=1), pltpu.roll(x, half, axis=1))
    return x * cos_t + partner * sin_t


def _proj_kernel(*refs, precise, rope_dim, has_ln, n_out):
    it = iter(refs)
    x_ref, w_ref = next(it), next(it)
    cos_ref = sin_ref = res_ref = g_ref = b_ref = None
    if rope_dim:
        cos_ref, sin_ref = next(it), next(it)
    if has_ln:
        res_ref, g_ref, b_ref = next(it), next(it), next(it)
    outs = [next(it) for _ in range(n_out)]
    acc = _dot(x_ref[...], w_ref[...], precise)
    if rope_dim:
        cos_t, sin_t = cos_ref[...], sin_ref[...]
        pieces = [_rope_lanes(acc[:, c * LANES:(c + 1) * LANES], cos_t, sin_t, rope_dim)
                  for c in range(acc.shape[1] // LANES)]
        acc = jnp.concatenate(pieces, axis=1) if len(pieces) > 1 else pieces[0]
    if has_ln:
        acc = _layer_norm_rows(DEEPNORM_ALPHA * res_ref[...] + acc, g_ref[...], b_ref[...])
    for o in outs:
        o[...] = acc.astype(o.dtype)


def _proj(x, w, *, col_start=0, n_cols=None, tm, tn, precise=False, rope=None, ln=None,
          out_dtypes=(F32,), rope_period=None, name="proj"):
    m, k = x.shape
    tm = min(tm, m)
    n_cols = w.shape[1] - col_start if n_cols is None else n_cols
    assert m % tm == 0 and n_cols % tn == 0 and col_start % tn == 0
    cb = col_start // tn
    grid = (n_cols // tn, m // tm)
    in_specs = [pl.BlockSpec((tm, k), lambda j, i: (i, 0)),
                pl.BlockSpec((k, tn), lambda j, i: (0, cb + j))]
    args = [x, w]
    rope_dim = 0
    if rope is not None:
        cos_t, sin_t, rope_dim = rope
        period = cos_t.shape[0] // tm
        in_specs += [pl.BlockSpec((tm, LANES), lambda j, i: (i % period, 0))] * 2
        args += [cos_t, sin_t]
    if ln is not None:
        assert tn == n_cols
        res, g, b = ln
        in_specs += [pl.BlockSpec((tm, tn), lambda j, i: (i, 0)),
                     pl.BlockSpec((1, tn), lambda j, i: (0, 0)),
                     pl.BlockSpec((1, tn), lambda j, i: (0, 0))]
        args += [res, g.reshape(1, -1), b.reshape(1, -1)]
    out_shape = [jax.ShapeDtypeStruct((m, n_cols), dt) for dt in out_dtypes]
    out_specs = [pl.BlockSpec((tm, tn), lambda j, i: (i, j)) for _ in out_dtypes]
    kern = functools.partial(_proj_kernel, precise=precise, rope_dim=rope_dim, has_ln=ln is not None,
                             n_out=len(out_dtypes))
    res = pl.pallas_call(kern, out_shape=out_shape, grid=grid, in_specs=in_specs, out_specs=out_specs,
                         compiler_params=_params("arbitrary", "arbitrary"), name=name)(*args)
    return res


def _lam_scalar(lam_ref, lam_init):
    v = lam_ref[...]
    a = jnp.sum(v[0:1] * v[1:2], axis=1, keepdims=True)
    b = jnp.sum(v[2:3] * v[3:4], axis=1, keepdims=True)
    return jnp.exp(a) - jnp.exp(b) + lam_init


def _subln(o, g, lam_init):
    ms = jnp.mean(o * o, axis=-1, keepdims=True)
    return o * lax.rsqrt(ms + LN_EPS) * g * (1.0 - lam_init)


def _diff_attn_kernel(lam_ref, g_ref, q_ref, k_ref, v_ref, o_ref, m_ref, l_ref, acc_ref, *, tq, lam_init):
    i = pl.program_id(2)
    hd = A_HEAD_DIM
    scale = hd ** -0.5
    m_ref[...] = jnp.full(m_ref.shape, NEG_BIG, F32)
    l_ref[...] = jnp.zeros(l_ref.shape, F32)
    acc_ref[...] = jnp.zeros(acc_ref.shape, F32)
    q = q_ref[0]
    row = i * tq + lax.broadcasted_iota(jnp.int32, (tq, tq), 0)
    col0 = lax.broadcasted_iota(jnp.int32, (tq, tq), 1)

    def body(j, carry):
        start = pl.multiple_of(j * tq, tq)
        kj = k_ref[0, pl.ds(start, tq), :]
        vj = v_ref[0, pl.ds(start, tq), :]
        causal = (j * tq + col0) <= row
        for c in range(2):
            s = _dot_nt(q[:, c * hd:(c + 1) * hd], kj[:, c * hd:(c + 1) * hd], False) * scale
            s = jnp.where(causal, s, NEG_BIG)
            m_old = m_ref[c]
            m_new = jnp.maximum(m_old, jnp.max(s, axis=1, keepdims=True))
            alpha = jnp.exp(m_old - m_new)
            p = jnp.exp(s - m_new)
            l_ref[c] = alpha * l_ref[c] + jnp.sum(p, axis=1, keepdims=True)
            acc_ref[c] = alpha * acc_ref[c] + _dot(p, vj, False)
            m_ref[c] = m_new
        return carry

    lax.fori_loop(0, i + 1, body, 0)
    lam = _lam_scalar(lam_ref, lam_init)
    o = acc_ref[0] / l_ref[0] - lam * (acc_ref[1] / l_ref[1])
    o_ref[0] = _subln(o, g_ref[...], lam_init).astype(o_ref.dtype)


def _diff_attn_prompt(q, k, v, lam_vecs, subln_g, lam_init, *, tq):
    b, t, _ = q.shape
    w = 2 * A_HEAD_DIM
    grid = (b, A_HEADS, t // tq)
    kern = functools.partial(_diff_attn_kernel, tq=tq, lam_init=lam_init)
    return pl.pallas_call(
        kern, out_shape=jax.ShapeDtypeStruct((b, t, A_HEADS * w), BF16), grid=grid,
        in_specs=[pl.BlockSpec((4, A_HEAD_DIM), lambda bi, h, i: (0, 0)),
                  pl.BlockSpec((1, w), lambda bi, h, i: (0, 0)),
                  pl.BlockSpec((1, tq, w), lambda bi, h, i: (bi, i, h)),
                  pl.BlockSpec((1, t, w), lambda bi, h, i: (bi, 0, h)),
                  pl.BlockSpec((1, t, w), lambda bi, h, i: (bi, 0, h))],
        out_specs=pl.BlockSpec((1, tq, w), lambda bi, h, i: (bi, i, h)),
        scratch_shapes=[pltpu.VMEM((2, tq, 1), F32), pltpu.VMEM((2, tq, 1), F32),
                        pltpu.VMEM((2, tq, w), F32)],
        compiler_params=_params("arbitrary", "arbitrary", "arbitrary"), name="diff_attn_prompt",
    )(lam_vecs, subln_g.reshape(1, w), q, k, v)


def _paged_attn_kernel(pt_ref, *refs, pages_per_step, scale, mode, lam_init):
    g_pages = pages_per_step
    it = iter(refs)
    q_ref = next(it)
    bias_ref = bias_new_ref = lam_ref = g_ref = None
    if mode == "dsa":
        bias_ref, bias_new_ref = next(it), next(it)
    else:
        lam_ref, g_ref = next(it), next(it)
    k_refs = [next(it) for _ in range(g_pages)]
    v_refs = [next(it) for _ in range(g_pages)]
    knew_ref, vnew_ref = next(it), next(it)
    o_ref = next(it)
    m_ref, l_ref, acc_ref = next(it), next(it), next(it)
    j = pl.program_id(1)
    n_steps = pl.num_programs(1)

    @pl.when(j == 0)
    def _():
        m_ref[...] = jnp.full(m_ref.shape, NEG_BIG, F32)
        l_ref[...] = jnp.zeros(l_ref.shape, F32)
        acc_ref[...] = jnp.zeros(acc_ref.shape, F32)

    q = q_ref[0]

    def update(s, v_rows):
        m_old = m_ref[...]
        m_new = jnp.maximum(m_old, jnp.max(s, axis=1, keepdims=True))
        alpha = jnp.exp(m_old - m_new)
        p = jnp.exp(s - m_new)
        l_ref[...] = alpha * l_ref[...] + jnp.sum(p, axis=1, keepdims=True)
        if v_rows.shape[0] == 1:
            pv = p * v_rows
        else:
            pv = _dot(p, v_rows, False)
        acc_ref[...] = alpha * acc_ref[...] + pv
        m_ref[...] = m_new

    for g in range(g_pages):
        s = _dot_nt(q, k_refs[g][0], False) * scale
        if mode == "dsa":
            s = s + bias_ref[0, pl.ds(j * g_pages + g, 1), :]
        update(s, v_refs[g][0])

    @pl.when(j == n_steps - 1)
    def _():
        s_new = jnp.sum(q * knew_ref[0], axis=1, keepdims=True) * scale
        if mode == "dsa":
            s_new = s_new + bias_new_ref[0]
        update(s_new, vnew_ref[0])
        o_full = acc_ref[...] / l_ref[...]
        outs = []
        if mode == "diff":
            lam = _lam_scalar(lam_ref, lam_init)
            w = 2 * A_HEAD_DIM
            for hp in range(A_HEADS):
                blk = o_full[:, hp * w:(hp + 1) * w]
                o = blk[2 * hp:2 * hp + 1] - lam * blk[2 * hp + 1:2 * hp + 2]
                outs.append(_subln(o, g_ref[...], lam_init))
        else:
            for h in range(B_HEADS):
                kv = h // B_GROUP
                outs.append(o_full[h:h + 1, kv * B_HEAD_DIM:(kv + 1) * B_HEAD_DIM])
        o_ref[0] = jnp.concatenate(outs, axis=1)


def _paged_attn(page_table, q_bd, cache_k, cache_v, k_new, v_new, *, mode, pages_per_step,
                bias=None, bias_new=None, lam_vecs=None, subln_g=None, lam_init=0.0):
    bs, n_heads, w = q_bd.shape
    n_pages = page_table.shape[1]
    g = pages_per_step
    assert n_pages % g == 0
    grid = (bs, n_pages // g)
    head_dim = A_HEAD_DIM if mode == "diff" else B_HEAD_DIM
    in_specs = [pl.BlockSpec((1, n_heads, w), lambda b, j, pt: (b, 0, 0))]
    args = [q_bd]
    if mode == "dsa":
        in_specs += [pl.BlockSpec((1, n_pages, PAGE_SIZE), lambda b, j, pt: (b, 0, 0)),
                     pl.BlockSpec((1, 1, 1), lambda b, j, pt: (b, 0, 0))]
        args += [bias, bias_new]
    else:
        in_specs += [pl.BlockSpec((4, A_HEAD_DIM), lambda b, j, pt: (0, 0)),
                     pl.BlockSpec((1, 2 * A_HEAD_DIM), lambda b, j, pt: (0, 0))]
        args += [lam_vecs, subln_g.reshape(1, -1)]

    def page_spec(gi):
        return pl.BlockSpec((1, PAGE_SIZE, w), lambda b, j, pt: (pt[b, j * g + gi], 0, 0))

    in_specs += [page_spec(gi) for gi in range(g)] + [page_spec(gi) for gi in range(g)]
    args += [cache_k] * g + [cache_v] * g
    in_specs += [pl.BlockSpec((1, 1, w), lambda b, j, pt: (b, 0, 0))] * 2
    args += [k_new, v_new]
    kern = functools.partial(_paged_attn_kernel, pages_per_step=g, scale=head_dim ** -0.5, mode=mode,
                             lam_init=lam_init)
    grid_spec = pltpu.PrefetchScalarGridSpec(
        num_scalar_prefetch=1, grid=grid, in_specs=in_specs,
        out_specs=pl.BlockSpec((1, 1, D_MODEL), lambda b, j, pt: (b, 0, 0)),
        scratch_shapes=[pltpu.VMEM((n_heads, 1), F32), pltpu.VMEM((n_heads, 1), F32),
                        pltpu.VMEM((n_heads, w), F32)])
    return pl.pallas_call(kern, out_shape=jax.ShapeDtypeStruct((bs, 1, D_MODEL), F32),
                          grid_spec=grid_spec, compiler_params=_params("arbitrary", "arbitrary"),
                          name="paged_attn_" + mode)(page_table, *args)


INT32_MIN = -2 ** 31
KEY_OF_NEG_INF = (-8388608) ^ 0x7FFFFFFF


def _topk_bias_into(bias_ref, scores, k_sel):
    r, n = scores.shape
    scores = jnp.where(scores == 0.0, 0.0, scores)
    bits = pltpu.bitcast(scores, jnp.int32)
    key = jnp.where(bits < 0, bits ^ 0x7FFFFFFF, bits)
    kf = float(k_sel)

    def step(t, ans):
        cand = ans + jnp.left_shift(jnp.int32(1), 31 - t)
        cnt = jnp.sum(jnp.where(key >= cand, 1.0, 0.0), axis=1, keepdims=True)
        return jnp.where(cnt >= kf, cand, ans)

    thr = lax.fori_loop(0, 32, step, jnp.full((r, 1), INT32_MIN, jnp.int32))
    gt = key > thr
    eq = key == thr
    cnt_gt = jnp.sum(jnp.where(gt, 1.0, 0.0), axis=1, keepdims=True)
    cnt_eq = jnp.sum(jnp.where(eq, 1.0, 0.0), axis=1, keepdims=True)
    bias_ref[...] = jnp.where(key >= thr, 0.0, NEG_BIG)
    tie = jnp.max(jnp.where((cnt_gt + cnt_eq > kf) & (thr > KEY_OF_NEG_INF), 1.0, 0.0))

    @pl.when(tie > 0.0)
    def _():
        need = kf - cnt_gt
        ri = lax.broadcasted_iota(jnp.int32, (LANES, LANES), 0)
        ci = lax.broadcasted_iota(jnp.int32, (LANES, LANES), 1)
        tri = jnp.where(ri < ci, 1.0, 0.0).astype(BF16)
        run = jnp.zeros((r, 1), F32)
        for c in range(n // LANES):
            sl = slice(c * LANES, (c + 1) * LANES)
            eq_c = jnp.where(eq[:, sl], 1.0, 0.0)
            before = jnp.dot(eq_c.astype(BF16), tri, preferred_element_type=F32) + run
            take = gt[:, sl] | (eq[:, sl] & (before < need))
            bias_ref[:, sl] = jnp.where(take, 0.0, NEG_BIG)
            run = run + jnp.sum(eq_c, axis=1, keepdims=True)


def _dsa_prompt_kernel(qi_ref, kibd_ref, wi_ref, q_ref, k_ref, v_ref, o_ref, bias_ref, *, tq, k_sel):
    i = pl.program_id(1)
    t = k_ref.shape[1]
    wi = wi_ref[0]
    qi = qi_ref[0]
    kibd = kibd_ref[0]
    score = jnp.zeros((tq, t), F32)
    for hp in range(IDX_HEADS // 2):
        d2 = _dot_nt(qi[:, hp * LANES:(hp + 1) * LANES], kibd, False)
        for c in range(2):
            h = 2 * hp + c
            score = score + wi[:, h:h + 1] * jnp.maximum(d2[:, c * t:(c + 1) * t], 0.0)
    row = i * tq + lax.broadcasted_iota(jnp.int32, (tq, t), 0)
    col = lax.broadcasted_iota(jnp.int32, (tq, t), 1)
    causal = col <= row
    _topk_bias_into(bias_ref, jnp.where(causal, score, -jnp.inf), k_sel)
    bias = jnp.where(causal, bias_ref[...], NEG_BIG)
    bias4 = jnp.concatenate([bias] * B_GROUP, axis=0)
    scale = B_HEAD_DIM ** -0.5
    q = q_ref[0]
    outs = []
    for kv in range(B_KV_HEADS):
        qg = jnp.concatenate([q[:, (kv * B_GROUP + r) * B_HEAD_DIM:(kv * B_GROUP + r + 1) * B_HEAD_DIM]
                              for r in range(B_GROUP)], axis=0)
        kg = k_ref[0, :, kv * B_HEAD_DIM:(kv + 1) * B_HEAD_DIM]
        vg = v_ref[0, :, kv * B_HEAD_DIM:(kv + 1) * B_HEAD_DIM]
        s = _dot_nt(qg, kg, False) * scale + bias4
        p = jnp.exp(s - jnp.max(s, axis=1, keepdims=True))
        l = jnp.sum(p, axis=1, keepdims=True)
        og = _dot(p, vg, False) / l
        outs += [og[r * tq:(r + 1) * tq] for r in range(B_GROUP)]
    o_ref[0] = jnp.concatenate(outs, axis=1).astype(o_ref.dtype)


def _dsa_prompt(qi, ki, wi, q, k, v, *, tq):
    b, t, _ = q.shape
    k_sel = min(IDX_TOPK_MAX, t // 4)
    zeros = jnp.zeros_like(ki)
    kibd = jnp.concatenate([jnp.concatenate([ki, zeros], axis=-1),
                            jnp.concatenate([zeros, ki], axis=-1)], axis=1)
    kvw = B_KV_HEADS * B_HEAD_DIM
    kern = functools.partial(_dsa_prompt_kernel, tq=tq, k_sel=k_sel)
    return pl.pallas_call(
        kern, out_shape=jax.ShapeDtypeStruct((b, t, D_MODEL), BF16), grid=(b, t // tq),
        in_specs=[pl.BlockSpec((1, tq, IDX_HEADS * IDX_DIM), lambda bi, i: (bi, i, 0)),
                  pl.BlockSpec((1, 2 * t, LANES), lambda bi, i: (bi, 0, 0)),
                  pl.BlockSpec((1, tq, IDX_HEADS), lambda bi, i: (bi, i, 0)),
                  pl.BlockSpec((1, tq, D_MODEL), lambda bi, i: (bi, i, 0)),
                  pl.BlockSpec((1, t, kvw), lambda bi, i: (bi, 0, 0)),
                  pl.BlockSpec((1, t, kvw), lambda bi, i: (bi, 0, 0))],
        out_specs=pl.BlockSpec((1, tq, D_MODEL), lambda bi, i: (bi, i, 0)),
        scratch_shapes=[pltpu.VMEM((tq, t), F32)],
        compiler_params=_params("arbitrary", "arbitrary"), name="dsa_prompt",
    )(qi, kibd, wi, q, k, v)


def _idx_scores_kernel(pt_ref, qi_ref, w_ref, *refs, pages_per_step):
    page_refs, o_ref = refs[:pages_per_step], refs[pages_per_step]
    qi = qi_ref[0]
    w = w_ref[0]
    rows = []
    for g in range(pages_per_step):
        d = _dot_nt(qi, page_refs[g][0], True)
        rows.append(jnp.sum(w * jnp.maximum(d, 0.0), axis=0, keepdims=True))
    o_ref[0] = jnp.concatenate(rows, axis=0)


def _idx_scores_paged(page_table, qi, wi, cache_kidx, *, pages_per_step):
    bs = qi.shape[0]
    n_pages = page_table.shape[1]
    g = pages_per_step

    def page_spec(gi):
        return pl.BlockSpec((1, PAGE_SIZE, IDX_DIM), lambda b, j, pt: (pt[b, j * g + gi], 0, 0))

    grid_spec = pltpu.PrefetchScalarGridSpec(
        num_scalar_prefetch=1, grid=(bs, n_pages // g),
        in_specs=[pl.BlockSpec((1, IDX_HEADS, IDX_DIM), lambda b, j, pt: (b, 0, 0)),
                  pl.BlockSpec((1, IDX_HEADS, 1), lambda b, j, pt: (b, 0, 0))]
        + [page_spec(gi) for gi in range(g)],
        out_specs=pl.BlockSpec((1, g, PAGE_SIZE), lambda b, j, pt: (b, j, 0)))
    kern = functools.partial(_idx_scores_kernel, pages_per_step=g)
    return pl.pallas_call(kern, out_shape=jax.ShapeDtypeStruct((bs, n_pages, PAGE_SIZE), F32),
                          grid_spec=grid_spec, compiler_params=_params("arbitrary", "arbitrary"),
                          name="dsa_idx_scores")(page_table, qi, wi, *([cache_kidx] * g))


def _topk_bias_kernel(s_ref, o_ref, *, k_sel):
    _topk_bias_into(o_ref, s_ref[...], k_sel)


def _topk_bias(scores, k_sel):
    return pl.pallas_call(functools.partial(_topk_bias_kernel, k_sel=k_sel),
                          out_shape=jax.ShapeDtypeStruct(scores.shape, F32),
                          compiler_params=pltpu.CompilerParams(vmem_limit_bytes=VMEM_LIMIT_BYTES),
                          name="topk_bias")(scores)


def _pool_prompt_kernel(x_ref, w_ref, sc_ref, o_ref, pad_ref, *, window, rows):
    t = x_ref.shape[1]
    halo = 2 * SUBLANES
    pad_ref[0:halo, :] = jnp.zeros((halo, pad_ref.shape[1]), F32)
    pad_ref[halo:, :] = x_ref[0]
    w = w_ref[0]
    sc = sc_ref[...]

    def body(c, carry):
        r0 = pl.multiple_of(c * rows, rows)
        xh = pad_ref[pl.ds(r0, rows + halo), :]
        x = xh[halo:, :]
        win = x
        for jj in range(1, window):
            win = win + xh[halo - jj:halo - jj + rows, :]
        pos = r0 + lax.broadcasted_iota(jnp.int32, (rows, 1), 0)
        count = jnp.minimum(window, pos + 1).astype(F32)
        mixed = win / count - x
        o_ref[0, pl.ds(r0, rows), :] = _dot(mixed, w, False) * sc
        return carry

    lax.fori_loop(0, t // rows, body, 0)


def _pool_prompt(x, w_group_bf16, scale):
    b, t, d = x.shape
    gd = POOL_GROUP_DIM
    outs = []
    for g, window in enumerate(POOL_WINDOWS):
        kern = functools.partial(_pool_prompt_kernel, window=window, rows=256)
        outs.append(pl.pallas_call(
            kern, out_shape=jax.ShapeDtypeStruct((b, t, gd), F32), grid=(b,),
            in_specs=[pl.BlockSpec((1, t, gd), lambda bi, g=g: (bi, 0, g)),
                      pl.BlockSpec((1, gd, gd), lambda bi, g=g: (g, 0, 0)),
                      pl.BlockSpec((1, gd), lambda bi, g=g: (0, g))],
            out_specs=pl.BlockSpec((1, t, gd), lambda bi: (bi, 0, 0)),
            scratch_shapes=[pltpu.VMEM((t + 2 * SUBLANES, gd), F32)],
            compiler_params=_params("arbitrary"), name="pool_prompt_w%d" % window,
        )(x, w_group_bf16, scale.reshape(1, d)))
    return jnp.concatenate(outs, axis=-1)


def _pool_sample_kernel(x_ref, w_ref, sc_ref, o_ref):
    g = pl.program_id(0)
    window = jnp.left_shift(jnp.int32(2), g)
    x = x_ref[...]
    n = x.shape[1]
    rowi = lax.broadcasted_iota(jnp.int32, x.shape, 1)
    win = jnp.sum(jnp.where(rowi >= n - window, x, 0.0), axis=1)
    mixed = win / window.astype(F32) - x[:, n - 1, :]
    o_ref[...] = _dot(mixed, w_ref[0], True) * sc_ref[...]


def _pool_sample(x_ext, w_group, scale):
    b, n, d = x_ext.shape
    gd = POOL_GROUP_DIM
    return pl.pallas_call(
        _pool_sample_kernel, out_shape=jax.ShapeDtypeStruct((b, d), F32), grid=(POOL_GROUPS,),
        in_specs=[pl.BlockSpec((b, n, gd), lambda g: (0, 0, g)),
                  pl.BlockSpec((1, gd, gd), lambda g: (g, 0, 0)),
                  pl.BlockSpec((1, gd), lambda g: (0, g))],
        out_specs=pl.BlockSpec((b, gd), lambda g: (0, g)),
        compiler_params=_params("arbitrary"), name="pool_sample",
    )(x_ext, w_group, scale.reshape(1, d))


def _router_kernel(x_ref, rw_ref, rb_ref, g_ref):
    tm = x_ref.shape[0]
    logits = _dot(x_ref[...], rw_ref[...], True)
    lane = lax.broadcasted_iota(jnp.int32, logits.shape, 1)
    logits = jnp.where(lane < N_EXPERTS, logits, NEG_BIG)
    e = jnp.exp(logits - jnp.max(logits, axis=1, keepdims=True))
    aff_t = (e / jnp.sum(e, axis=1, keepdims=True)).T
    bias = rb_ref[...]
    aff = [aff_t[x:x + 1, :] for x in range(N_EXPERTS)]
    sel = [aff[x] + bias[x:x + 1, :] for x in range(N_EXPERTS)]
    gsz = EXPERTS_PER_GROUP
    best_score = best_group = None
    for g in range(N_EXPERT_GROUPS):
        a, b, c, d = sel[g * gsz:(g + 1) * gsz]
        hi1, lo1, hi2, lo2 = jnp.maximum(a, b), jnp.minimum(a, b), jnp.maximum(c, d), jnp.minimum(c, d)
        score = jnp.maximum(hi1, hi2) + jnp.maximum(jnp.minimum(hi1, hi2), jnp.maximum(lo1, lo2))
        if g == 0:
            best_score, best_group = score, jnp.zeros(score.shape, jnp.int32)
        else:
            better = score > best_score
            best_group = jnp.where(better, g, best_group)
            best_score = jnp.where(better, score, best_score)
    chosen = []
    for x in range(N_EXPERTS):
        g = x // gsz
        rank = jnp.zeros((1, tm), F32)
        for y in range(g * gsz, (g + 1) * gsz):
            if y == x:
                continue
            ahead = (sel[y] > sel[x]) | ((sel[y] == sel[x]) & (y < x))
            rank = rank + jnp.where(ahead, 1.0, 0.0)
        chosen.append((best_group == g) & (rank < 2.0))
    top_sum = jnp.zeros((1, tm), F32)
    for x in range(N_EXPERTS):
        top_sum = top_sum + jnp.where(chosen[x], aff[x], 0.0)
    rows = [jnp.where(chosen[x], aff[x] / top_sum, 0.0) for x in range(N_EXPERTS)]
    rows.append(jnp.zeros((LANES - N_EXPERTS, tm), F32))
    g_ref[...] = jnp.concatenate(rows, axis=0).T


def _router(x, router_w, router_bias, *, tm):
    m, d = x.shape
    tm = min(tm, m)
    rw = jnp.pad(router_w, ((0, 0), (0, LANES - N_EXPERTS)))
    return pl.pallas_call(
        _router_kernel, out_shape=jax.ShapeDtypeStruct((m, LANES), F32), grid=(m // tm,),
        in_specs=[pl.BlockSpec((tm, d), lambda i: (i, 0)),
                  pl.BlockSpec((d, LANES), lambda i: (0, 0)),
                  pl.BlockSpec((N_EXPERTS, 1), lambda i: (0, 0))],
        out_specs=pl.BlockSpec((tm, LANES), lambda i: (i, 0)),
        compiler_params=_params("arbitrary"), name="router",
    )(x, rw, router_bias.reshape(N_EXPERTS, 1))


def _moe_dense_kernel(x_ref, g_ref, win_ref, wdn_ref, res_ref, lg_ref, lb_ref, *refs, precise):
    out_refs, acc_ref = refs[:-1], refs[-1]
    e = pl.program_id(1)

    @pl.when(e == 0)
    def _():
        acc_ref[...] = jnp.zeros(acc_ref.shape, F32)

    h = _dot(x_ref[...], win_ref[0], precise)
    hg, hu = h[:, :D_EXPERT], h[:, D_EXPERT:]
    gates = g_ref[...]
    lane = lax.broadcasted_iota(jnp.int32, gates.shape, 1)
    gate = jnp.sum(jnp.where(lane == e, gates, 0.0), axis=1, keepdims=True)
    a = hg * jax.nn.sigmoid(hg) * hu * gate
    acc_ref[...] += _dot(a, wdn_ref[0], precise)

    @pl.when(e == pl.num_programs(1) - 1)
    def _():
        y = _layer_norm_rows(DEEPNORM_ALPHA * res_ref[...] + acc_ref[...], lg_ref[...], lb_ref[...])
        for o in out_refs:
            o[...] = y.astype(o.dtype)


def _moe_dense(x, gates, w_in, w_down, res, ln_g, ln_b, *, tm, precise, out_dtypes):
    m, d = x.shape
    tm = min(tm, m)
    kern = functools.partial(_moe_dense_kernel, precise=precise)
    return pl.pallas_call(
        kern, out_shape=[jax.ShapeDtypeStruct((m, d), dt) for dt in out_dtypes],
        grid=(m // tm, N_EXPERTS),
        in_specs=[pl.BlockSpec((tm, d), lambda i, e: (i, 0)),
                  pl.BlockSpec((tm, LANES), lambda i, e: (i, 0)),
                  pl.BlockSpec((1, d, 2 * D_EXPERT), lambda i, e: (e, 0, 0)),
                  pl.BlockSpec((1, D_EXPERT, d), lambda i, e: (e, 0, 0)),
                  pl.BlockSpec((tm, d), lambda i, e: (i, 0)),
                  pl.BlockSpec((1, d), lambda i, e: (0, 0)),
                  pl.BlockSpec((1, d), lambda i, e: (0, 0))],
        out_specs=[pl.BlockSpec((tm, d), lambda i, e: (i, 0)) for _ in out_dtypes],
        scratch_shapes=[pltpu.VMEM((tm, d), F32)],
        compiler_params=_params("arbitrary", "arbitrary"), name="moe_dense",
    )(x, gates, w_in, w_down, res, ln_g.reshape(1, d), ln_b.reshape(1, d))


def _post_norm_kernel(x_ref, sub_ref, g_ref, b_ref, *out_refs):
    y = _layer_norm_rows(DEEPNORM_ALPHA * x_ref[...] + sub_ref[...], g_ref[...], b_ref[...])
    for o in out_refs:
        o[...] = y.astype(o.dtype)


def _post_norm(x, sub, g, b, *, tm, out_dtypes):
    m, d = x.shape
    tm = min(tm, m)
    row = pl.BlockSpec((tm, d), lambda i: (i, 0))
    vec = pl.BlockSpec((1, d), lambda i: (0, 0))
    return pl.pallas_call(
        _post_norm_kernel, out_shape=[jax.ShapeDtypeStruct((m, d), dt) for dt in out_dtypes],
        grid=(m // tm,), in_specs=[row, row, vec, vec], out_specs=[row for _ in out_dtypes],
        compiler_params=_params("arbitrary"), name="post_norm",
    )(x, sub, g.reshape(1, d), b.reshape(1, d))


def _block_diag_queries(q, n_kv_heads):
    bs, h, d = q.shape
    kv_of_head = jnp.arange(h) // (h // n_kv_heads)
    onehot = (kv_of_head[:, None] == jnp.arange(n_kv_heads)[None, :]).astype(q.dtype)
    return (q[:, :, None, :] * onehot[None, :, :, None]).reshape(bs, h, n_kv_heads * d)


PROMPT_TM = 1024
PROMPT_LN_TM = 256
MOE_TM = 512
ATTN_TQ = 256
DSA_TQ = 128
DIFF_PAGES_PER_STEP = 4
DSA_PAGES_PER_STEP = 8


def _moe_prompt(xf, xb, router_w, router_bias, w_in, w_down, ln_g, ln_b):
    gates = _router(xf, router_w, router_bias, tm=PROMPT_TM)
    return _moe_dense(xb, gates, w_in.astype(BF16), w_down.astype(BF16), xf, ln_g, ln_b,
                      tm=MOE_TM, precise=False, out_dtypes=(F32, BF16))


def _moe_sample(xs, router_w, router_bias, w_in, w_down, ln_g, ln_b):
    n = xs.shape[0]
    xpad = jnp.pad(xs, ((0, LANES - n), (0, 0)))
    gates = _router(xpad, router_w, router_bias, tm=LANES)[:n]
    (y,) = _moe_dense(xs, gates, w_in, w_down, xs, ln_g, ln_b, tm=n, precise=True, out_dtypes=(F32,))
    return y


def _diff_layer(xpf, xpb, xs, cache_k, cache_v, page_table, w_qkv, w_o, lq1, lk1, lq2, lk2, subln_g,
                ln_g, ln_b, layer_idx, b, t):
    lam_init = 0.8 - 0.6 * math.exp(-0.3 * layer_idx)
    lam_vecs = jnp.stack([lq1, lk1, lq2, lk2]).astype(F32)
    dqk = 2 * A_HEADS * A_HEAD_DIM
    cos_p, sin_p = _rope_tables(jnp.arange(t, dtype=jnp.int32), A_HEAD_DIM)
    wb = w_qkv.astype(BF16)
    tm = PROMPT_TM
    (qb,) = _proj(xpb, wb, col_start=0, n_cols=dqk, tm=tm, tn=1024, rope=(cos_p, sin_p, A_HEAD_DIM),
                  out_dtypes=(BF16,), name="diff_q")
    kf, kb = _proj(xpb, wb, col_start=dqk, n_cols=dqk, tm=tm, tn=1024, rope=(cos_p, sin_p, A_HEAD_DIM),
                   out_dtypes=(F32, BF16), name="diff_k")
    vf, vb = _proj(xpb, wb, col_start=2 * dqk, n_cols=dqk, tm=tm, tn=1024, out_dtypes=(F32, BF16),
                   name="diff_v")
    o = _diff_attn_prompt(qb.reshape(b, t, dqk), kb.reshape(b, t, dqk), vb.reshape(b, t, dqk),
                          lam_vecs, subln_g, lam_init, tq=ATTN_TQ)
    x1f, x1b = _proj(o.reshape(b * t, dqk), w_o.astype(BF16), tm=PROMPT_LN_TM, tn=D_MODEL,
                     ln=(xpf, ln_g, ln_b), out_dtypes=(F32, BF16), name="diff_wo")
    bs = xs.shape[0]
    pos_s = jnp.full((bs,), PAST_LEN, jnp.int32)
    cos_s, sin_s = _rope_tables(pos_s, A_HEAD_DIM)
    (qs,) = _proj(xs, w_qkv, col_start=0, n_cols=dqk, tm=bs, tn=1024, precise=True,
                  rope=(cos_s, sin_s, A_HEAD_DIM), name="diff_q_s")
    (ks,) = _proj(xs, w_qkv, col_start=dqk, n_cols=dqk, tm=bs, tn=1024, precise=True,
                  rope=(cos_s, sin_s, A_HEAD_DIM), name="diff_k_s")
    (vs,) = _proj(xs, w_qkv, col_start=2 * dqk, n_cols=dqk, tm=bs, tn=1024, precise=True, name="diff_v_s")
    n_pool = cache_k.shape[0]
    q_bd = _block_diag_queries(qs.reshape(bs, 2 * A_HEADS, A_HEAD_DIM), 2 * A_HEADS)
    os_ = _paged_attn(page_table, q_bd, cache_k.reshape(n_pool, PAGE_SIZE, dqk),
                      cache_v.reshape(n_pool, PAGE_SIZE, dqk), ks.reshape(bs, 1, dqk), vs.reshape(bs, 1, dqk),
                      mode="diff", pages_per_step=DIFF_PAGES_PER_STEP, lam_vecs=lam_vecs, subln_g=subln_g,
                      lam_init=lam_init)
    (xs1,) = _proj(os_.reshape(bs, dqk), w_o, tm=bs, tn=D_MODEL, precise=True, ln=(xs, ln_g, ln_b),
                   name="diff_wo_s")
    state = (kf.reshape(b, t, 2 * A_HEADS, A_HEAD_DIM), vf.reshape(b, t, A_HEADS, 2 * A_HEAD_DIM),
             ks.reshape(bs, 1, 2 * A_HEADS, A_HEAD_DIM), vs.reshape(bs, 1, A_HEADS, 2 * A_HEAD_DIM))
    return x1f, x1b, xs1, state


def _dsa_layer(xpf, xpb, xs, cache_k, cache_v, cache_kidx, page_table, w_in, w_o, ln_g, ln_b, b, t):
    dq = B_HEADS * B_HEAD_DIM
    dkv = B_KV_HEADS * B_HEAD_DIM
    dqi = IDX_HEADS * IDX_DIM
    c_k, c_v, c_qi, c_tail = dq, dq + dkv, dq + 2 * dkv, dq + 2 * dkv + dqi
    w_tail = jnp.pad(w_in[:, c_tail:], ((0, 0), (0, LANES - (IDX_DIM + IDX_HEADS))))

    def project(x, w, wt, tm, precise, cos_h, sin_h, cos_i, sin_i, tag):
        both = (F32,) if precise else (F32, BF16)
        low = (F32,) if precise else (BF16,)
        kw = dict(tm=tm, precise=precise)
        q = _proj(x, w, col_start=0, n_cols=dq, tn=1024, rope=(cos_h, sin_h, B_HEAD_DIM),
                  out_dtypes=low, name="dsa_q" + tag, **kw)
        k = _proj(x, w, col_start=c_k, n_cols=dkv, tn=dkv, rope=(cos_h, sin_h, B_HEAD_DIM),
                  out_dtypes=both, name="dsa_k" + tag, **kw)
        v = _proj(x, w, col_start=c_v, n_cols=dkv, tn=dkv, out_dtypes=both, name="dsa_v" + tag, **kw)
        qi = _proj(x, w, col_start=c_qi, n_cols=dqi, tn=dqi, rope=(cos_i, sin_i, IDX_DIM),
                   out_dtypes=low, name="dsa_qi" + tag, **kw)
        ki = _proj(x, wt, tn=LANES, rope=(cos_i, sin_i, IDX_DIM), out_dtypes=both,
                   name="dsa_ki" + tag, **kw)
        (wi,) = _proj(x, wt, tn=LANES, name="dsa_wi" + tag, **kw)
        return q, k, v, qi, ki, wi[:, IDX_DIM:IDX_DIM + IDX_HEADS]

    pos_p = jnp.arange(t, dtype=jnp.int32)
    q, k, v, qi, ki, wi = project(xpb, w_in.astype(BF16), w_tail.astype(BF16), PROMPT_TM, False,
                                  *_rope_tables(pos_p, B_HEAD_DIM), *_rope_tables(pos_p, IDX_DIM), "")
    o = _dsa_prompt(qi[0].reshape(b, t, dqi), ki[1][:, :IDX_DIM].reshape(b, t, IDX_DIM),
                    wi.reshape(b, t, IDX_HEADS), q[0].reshape(b, t, dq), k[1].reshape(b, t, dkv),
                    v[1].reshape(b, t, dkv), tq=DSA_TQ)
    x1f, x1b = _proj(o.reshape(b * t, dq), w_o.astype(BF16), tm=PROMPT_LN_TM, tn=D_MODEL,
                     ln=(xpf, ln_g, ln_b), out_dtypes=(F32, BF16), name="dsa_wo")
    bs = xs.shape[0]
    pos_s = jnp.full((bs,), PAST_LEN, jnp.int32)
    qs, ks, vs, qis, kis, wis = project(xs, w_in, w_tail, bs, True, *_rope_tables(pos_s, B_HEAD_DIM),
                                        *_rope_tables(pos_s, IDX_DIM), "_s")
    qs, ks, vs, qis, kis = qs[0], ks[0], vs[0], qis[0], kis[0][:, :IDX_DIM]
    qis3 = qis.reshape(bs, IDX_HEADS, IDX_DIM)
    wis3 = wis.reshape(bs, IDX_HEADS, 1)
    n_pages = page_table.shape[1]
    past_scores = _idx_scores_paged(page_table, qis3, wis3, cache_kidx, pages_per_step=DSA_PAGES_PER_STEP)
    new_page = jnp.pad(kis.reshape(bs, 1, IDX_DIM), ((0, 0), (0, PAGE_SIZE - 1), (0, 0)))
    new_scores = _idx_scores_paged(jnp.arange(bs, dtype=jnp.int32).reshape(bs, 1), qis3, wis3, new_page,
                                   pages_per_step=1)
    n_past = n_pages * PAGE_SIZE
    all_scores = jnp.concatenate(
        [past_scores.reshape(bs, n_past), new_scores[:, 0, :1], jnp.full((bs, LANES - 1), -jnp.inf, F32)], axis=1)
    bias_all = _topk_bias(all_scores, min(IDX_TOPK_MAX, (n_past + 1) // 4))
    n_pool = cache_k.shape[0]
    q_bd = _block_diag_queries(qs.reshape(bs, B_HEADS, B_HEAD_DIM), B_KV_HEADS)
    os_ = _paged_attn(page_table, q_bd, cache_k.reshape(n_pool, PAGE_SIZE, dkv),
                      cache_v.reshape(n_pool, PAGE_SIZE, dkv), ks.reshape(bs, 1, dkv), vs.reshape(bs, 1, dkv),
                      mode="dsa", pages_per_step=DSA_PAGES_PER_STEP,
                      bias=bias_all[:, :n_past].reshape(bs, n_pages, PAGE_SIZE),
                      bias_new=bias_all[:, n_past:n_past + 1].reshape(bs, 1, 1))
    (xs1,) = _proj(os_.reshape(bs, dq), w_o, tm=bs, tn=D_MODEL, precise=True, ln=(xs, ln_g, ln_b),
                   name="dsa_wo_s")
    state = (k[0].reshape(b, t, B_KV_HEADS, B_HEAD_DIM), v[0].reshape(b, t, B_KV_HEADS, B_HEAD_DIM),
             ki[0][:, :IDX_DIM].reshape(b, t, IDX_DIM),
             ks.reshape(bs, 1, B_KV_HEADS, B_HEAD_DIM), vs.reshape(bs, 1, B_KV_HEADS, B_HEAD_DIM),
             kis.reshape(bs, 1, IDX_DIM))
    return x1f, x1b, xs1, state


def _pool_layer(xpf, xs, state_pool, w_group, scale, ln_g, ln_b, b, t):
    xp3 = xpf.reshape(b, t, D_MODEL)
    sub = _pool_prompt(xp3, w_group.astype(BF16), scale)
    x1f, x1b = _post_norm(xpf, sub.reshape(b * t, D_MODEL), ln_g, ln_b, tm=PROMPT_LN_TM,
                          out_dtypes=(F32, BF16))
    bs = xs.shape[0]
    xs_ext = jnp.concatenate([state_pool.astype(xs.dtype), xs.reshape(bs, 1, D_MODEL)], axis=1)
    sub_s = _pool_sample(xs_ext, w_group, scale)
    (xs1,) = _post_norm(xs, sub_s, ln_g, ln_b, tm=bs, out_dtypes=(F32,))
    return x1f, x1b, xs1, (xp3[:, -POOL_STATE_LEN:], xs_ext[:, -POOL_STATE_LEN:])


def kernel(x_prompt, x_sample, cache_l0_k, cache_l0_v, cache_l1_k, cache_l1_v, cache_l1_kidx, state_l2_pool, cache_l3_k, cache_l3_v, page_table, router_w, router_bias, l0_w_qkv, l0_w_o, l0_lam_q1, l0_lam_k1, l0_lam_q2, l0_lam_k2, l0_subln_g, l0_ln1_g, l0_ln1_b, l0_moe_w_in, l0_moe_w_down, l0_ln2_g, l0_ln2_b, l1_w_in, l1_w_o, l1_ln1_g, l1_ln1_b, l1_moe_w_in, l1_moe_w_down, l1_ln2_g, l1_ln2_b, l2_w_group, l2_scale, l2_ln1_g, l2_ln1_b, l2_moe_w_in, l2_moe_w_down, l2_ln2_g, l2_ln2_b, l3_w_qkv, l3_w_o, l3_lam_q1, l3_lam_k1, l3_lam_q2, l3_lam_k2, l3_subln_g, l3_ln1_g, l3_ln1_b, l3_moe_w_in, l3_moe_w_down, l3_ln2_g, l3_ln2_b):
    b, t, d = x_prompt.shape
    bs = x_sample.shape[0]
    xpf = x_prompt.reshape(b * t, d)
    xpb = xpf.astype(BF16)
    xs = x_sample.reshape(bs, d)
    moe_p = [(l0_moe_w_in, l0_moe_w_down, l0_ln2_g, l0_ln2_b), (l1_moe_w_in, l1_moe_w_down, l1_ln2_g, l1_ln2_b),
             (l2_moe_w_in, l2_moe_w_down, l2_ln2_g, l2_ln2_b), (l3_moe_w_in, l3_moe_w_down, l3_ln2_g, l3_ln2_b)]
    states = []
    for i in range(DEPTH):
        if i == 0:
            xpf, xpb, xs, st = _diff_layer(xpf, xpb, xs, cache_l0_k, cache_l0_v, page_table, l0_w_qkv, l0_w_o,
                                           l0_lam_q1, l0_lam_k1, l0_lam_q2, l0_lam_k2, l0_subln_g,
                                           l0_ln1_g, l0_ln1_b, i, b, t)
        elif i == 1:
            xpf, xpb, xs, st = _dsa_layer(xpf, xpb, xs, cache_l1_k, cache_l1_v, cache_l1_kidx, page_table,
                                          l1_w_in, l1_w_o, l1_ln1_g, l1_ln1_b, b, t)
        elif i == 2:
            xpf, xpb, xs, st = _pool_layer(xpf, xs, state_l2_pool, l2_w_group, l2_scale, l2_ln1_g, l2_ln1_b, b, t)
        else:
            xpf, xpb, xs, st = _diff_layer(xpf, xpb, xs, cache_l3_k, cache_l3_v, page_table, l3_w_qkv, l3_w_o,
                                           l3_lam_q1, l3_lam_k1, l3_lam_q2, l3_lam_k2, l3_subln_g,
                                           l3_ln1_g, l3_ln1_b, i, b, t)
        states.append(st)
        w_in, w_down, ln_g, ln_b = moe_p[i]
        xpf, xpb = _moe_prompt(xpf, xpb, router_w, router_bias, w_in, w_down, ln_g, ln_b)
        xs = _moe_sample(xs, router_w, router_bias, w_in, w_down, ln_g, ln_b)
    l0, l1, l2, l3 = states
    return (xpf.reshape(b, t, d), xs.reshape(bs, 1, d),
            l0[0], l0[1], l0[2], l0[3],
            l1[0], l1[1], l1[2], l1[3], l1[4], l1[5],
            l2[0], l2[1],
            l3[0], l3[1], l3[2], l3[3])
```

```python
import functools
import math

import jax
import jax.numpy as jnp
from jax import lax
from jax.experimental import pallas as pl
from jax.experimental.pallas import tpu as pltpu

F32 = jnp.float32
BF16 = jnp.bfloat16

D_MODEL = 2048
DEPTH = 4
PAST_LEN = 16384
PAGE_SIZE = 128
ROPE_THETA = 10000.0
LN_EPS = 1e-5
A_HEADS = 8
A_HEAD_DIM = D_MODEL // A_HEADS // 2
B_HEADS = 16
B_HEAD_DIM = D_MODEL // B_HEADS
B_KV_HEADS = 4
B_GROUP = B_HEADS // B_KV_HEADS
IDX_HEADS = 16
IDX_DIM = 64
IDX_TOPK_MAX = 256
POOL_WINDOWS = (2, 4, 8, 16)
POOL_GROUPS = 4
POOL_GROUP_DIM = D_MODEL // POOL_GROUPS
POOL_STATE_LEN = max(POOL_WINDOWS) - 1
N_EXPERTS = 16
N_EXPERT_GROUPS = 4
EXPERTS_PER_GROUP = N_EXPERTS // N_EXPERT_GROUPS
D_EXPERT = D_MODEL // 4
DEEPNORM_ALPHA = (2 * DEPTH) ** 0.25

LANES = 128
SUBLANES = 8
VMEM_LIMIT_BYTES = 56 * 1024 * 1024

NEG_BIG = -1e30
HIGHEST = lax.Precision.HIGHEST


def _params(*sem):
    return pltpu.CompilerParams(dimension_semantics=sem, vmem_limit_bytes=VMEM_LIMIT_BYTES)


def _dot(a, b, precise):
    if precise:
        return jnp.dot(a.astype(F32), b.astype(F32), precision=HIGHEST, preferred_element_type=F32)
    return jnp.dot(a.astype(BF16), b.astype(BF16), preferred_element_type=F32)


def _dot_nt(a, b, precise):
    dims = (((1,), (1,)), ((), ()))
    if precise:
        return lax.dot_general(a.astype(F32), b.astype(F32), dims, precision=HIGHEST,
                               preferred_element_type=F32)
    return lax.dot_general(a.astype(BF16), b.astype(BF16), dims, preferred_element_type=F32)


def _layer_norm_rows(r, g, b):
    mu = jnp.mean(r, axis=-1, keepdims=True)
    c = r - mu
    var = jnp.mean(c * c, axis=-1, keepdims=True)
    return c * lax.rsqrt(var + LN_EPS) * g + b


def _rope_tables(pos, head_dim):
    half = head_dim // 2
    inv_freq = ROPE_THETA ** (-jnp.arange(half, dtype=F32) * 2.0 / head_dim)
    ang = pos.astype(F32)[:, None] * inv_freq[None, :]
    cos, sin = jnp.cos(ang), jnp.sin(ang)
    reps = LANES // head_dim
    cos_t = jnp.tile(jnp.concatenate([cos, cos], axis=-1), (1, reps))
    sin_t = jnp.tile(jnp.concatenate([-sin, sin], axis=-1), (1, reps))
    return cos_t, sin_t


def _rope_lanes(x, cos_t, sin_t, head_dim):
    half = head_dim // 2
    if head_dim == LANES:
        partner = pltpu.roll(x, half, axis=1)
    else:
        lane = lax.broadcasted_iota(jnp.int32, x.shape, 1)
        first = (lane % head_dim) < half
        partner = jnp.where(first, pltpu.roll(x, LANES - half, axis=1), pltpu.roll(x, half, axis=1))
    return x * cos_t + partner * sin_t


def _proj_kernel(*refs, precise, rope_dim, has_ln, n_out, out_scale):
    it = iter(refs)
    x_ref, w_ref = next(it), next(it)
    cos_ref = sin_ref = res_ref = g_ref = b_ref = None
    if rope_dim:
        cos_ref, sin_ref = next(it), next(it)
    if has_ln:
        res_ref, g_ref, b_ref = next(it), next(it), next(it)
    outs = [next(it) for _ in range(n_out)]
    acc = _dot(x_ref[...], w_ref[...], precise)
    if rope_dim:
        cos_t, sin_t = cos_ref[...], sin_ref[...]
        pieces = [_rope_lanes(acc[:, c * LANES:(c + 1) * LANES], cos_t, sin_t, rope_dim)
                  for c in range(acc.shape[1] // LANES)]
        acc = jnp.concatenate(pieces, axis=1) if len(pieces) > 1 else pieces[0]
    if has_ln:
        acc = _layer_norm_rows(DEEPNORM_ALPHA * res_ref[...] + acc, g_ref[...], b_ref[...])
    if out_scale != 1.0:
        acc = acc * out_scale
    for o in outs:
        o[...] = acc.astype(o.dtype)


def _proj(x, w, *, col_start=0, n_cols=None, tm, tn, precise=False, rope=None, ln=None,
          out_dtypes=(F32,), out_scale=1.0, name="proj"):
    m, k = x.shape
    tm = min(tm, m)
    n_cols = w.shape[1] - col_start if n_cols is None else n_cols
    assert m % tm == 0 and n_cols % tn == 0 and col_start % tn == 0
    cb = col_start // tn
    grid = (n_cols // tn, m // tm)
    in_specs = [pl.BlockSpec((tm, k), lambda j, i: (i, 0)),
                pl.BlockSpec((k, tn), lambda j, i: (0, cb + j))]
    args = [x, w]
    rope_dim = 0
    if rope is not None:
        cos_t, sin_t, rope_dim = rope
        period = cos_t.shape[0] // tm
        in_specs += [pl.BlockSpec((tm, LANES), lambda j, i: (i % period, 0))] * 2
        args += [cos_t, sin_t]
    if ln is not None:
        assert tn == n_cols
        res, g, b = ln
        in_specs += [pl.BlockSpec((tm, tn), lambda j, i: (i, 0)),
                     pl.BlockSpec((1, tn), lambda j, i: (0, 0)),
                     pl.BlockSpec((1, tn), lambda j, i: (0, 0))]
        args += [res, g.reshape(1, -1), b.reshape(1, -1)]
    out_shape = [jax.ShapeDtypeStruct((m, n_cols), dt) for dt in out_dtypes]
    out_specs = [pl.BlockSpec((tm, tn), lambda j, i: (i, j)) for _ in out_dtypes]
    kern = functools.partial(_proj_kernel, precise=precise, rope_dim=rope_dim, has_ln=ln is not None,
                             n_out=len(out_dtypes), out_scale=out_scale)
    res = pl.pallas_call(kern, out_shape=out_shape, grid=grid, in_specs=in_specs, out_specs=out_specs,
                         compiler_params=_params("arbitrary", "arbitrary"), name=name)(*args)
    return res


def _lam_scalar(lam_ref, lam_init):
    v = lam_ref[...]
    a = jnp.sum(v[0:1] * v[1:2], axis=1, keepdims=True)
    b = jnp.sum(v[2:3] * v[3:4], axis=1, keepdims=True)
    return jnp.exp(a) - jnp.exp(b) + lam_init


def _subln(o, g, lam_init):
    ms = jnp.mean(o * o, axis=-1, keepdims=True)
    return o * lax.rsqrt(ms + LN_EPS) * g * (1.0 - lam_init)


def _diff_attn_kernel(lam_ref, g_ref, q_ref, k_ref, v_ref, o_ref, *, tq, lam_init, n_blocks):
    i = pl.program_id(2)
    hd = A_HEAD_DIM
    lam = _lam_scalar(lam_ref, lam_init)
    row = lax.broadcasted_iota(jnp.int32, (tq, tq), 0)
    col = lax.broadcasted_iota(jnp.int32, (tq, tq), 1)
    on_or_below_diag = col <= row

    def branch(c):
        n_left = c * tq
        q = q_ref[0]
        streams = []
        for s_ in range(2):
            lanes = slice(s_ * hd, (s_ + 1) * hd)
            qs = q[:, lanes]
            sd = _dot_nt(qs, k_ref[0, n_left:n_left + tq, lanes], False)
            sd = jnp.where(on_or_below_diag, sd, NEG_BIG)
            m = jnp.max(sd, axis=1, keepdims=True)
            if c > 0:
                sl = _dot_nt(qs, k_ref[0, 0:n_left, lanes], False)
                m = jnp.maximum(m, jnp.max(sl, axis=1, keepdims=True))
            pd = jnp.exp2(sd - m)
            l = jnp.sum(pd, axis=1, keepdims=True)
            o = _dot(pd, v_ref[0, n_left:n_left + tq, :], False)
            if c > 0:
                p_left = jnp.exp2(sl - m)
                l = l + jnp.sum(p_left, axis=1, keepdims=True)
                o = o + _dot(p_left, v_ref[0, 0:n_left, :], False)
            streams.append(o / l)
        o = streams[0] - lam * streams[1]
        o_ref[0] = _subln(o, g_ref[...], lam_init).astype(o_ref.dtype)

    for c in range(n_blocks):
        pl.when(i == c)(functools.partial(branch, c))


LOG2_E = 1.4426950408889634


def _diff_attn_prompt(q, k, v, lam_vecs, subln_g, lam_init, *, tq):
    b, t, _ = q.shape
    w = 2 * A_HEAD_DIM
    grid = (b, A_HEADS, t // tq)
    kern = functools.partial(_diff_attn_kernel, tq=tq, lam_init=lam_init, n_blocks=t // tq)
    return pl.pallas_call(
        kern, out_shape=jax.ShapeDtypeStruct((b, t, A_HEADS * w), BF16), grid=grid,
        in_specs=[pl.BlockSpec((4, A_HEAD_DIM), lambda bi, h, i: (0, 0)),
                  pl.BlockSpec((1, w), lambda bi, h, i: (0, 0)),
                  pl.BlockSpec((1, tq, w), lambda bi, h, i: (bi, i, h)),
                  pl.BlockSpec((1, t, w), lambda bi, h, i: (bi, 0, h)),
                  pl.BlockSpec((1, t, w), lambda bi, h, i: (bi, 0, h))],
        out_specs=pl.BlockSpec((1, tq, w), lambda bi, h, i: (bi, i, h)),
        compiler_params=_params("arbitrary", "arbitrary", "arbitrary"), name="diff_attn_prompt",
    )(lam_vecs, subln_g.reshape(1, w), q, k, v)


def _paged_attn_kernel(pt_ref, *refs, pages_per_step, scale, mode, lam_init):
    g_pages = pages_per_step
    it = iter(refs)
    q_ref = next(it)
    bias_ref = bias_new_ref = lam_ref = g_ref = None
    if mode == "dsa":
        bias_ref, bias_new_ref = next(it), next(it)
    else:
        lam_ref, g_ref = next(it), next(it)
    k_refs = [next(it) for _ in range(g_pages)]
    v_refs = [next(it) for _ in range(g_pages)]
    knew_ref, vnew_ref = next(it), next(it)
    o_ref = next(it)
    m_ref, l_ref, acc_ref = next(it), next(it), next(it)
    j = pl.program_id(1)
    n_steps = pl.num_programs(1)

    @pl.when(j == 0)
    def _():
        m_ref[...] = jnp.full(m_ref.shape, NEG_BIG, F32)
        l_ref[...] = jnp.zeros(l_ref.shape, F32)
        acc_ref[...] = jnp.zeros(acc_ref.shape, F32)

    q = q_ref[0]
    n_rows = q.shape[0]
    if mode == "diff":
        lanes = PAGE_SIZE * A_HEADS
        wanted = lambda r, lane: (lane % A_HEADS) == (r % A_HEADS)
    else:
        lanes = PAGE_SIZE * B_KV_HEADS
        wanted = lambda r, lane: (lane % B_KV_HEADS) == (r // B_GROUP)
    ri = lax.broadcasted_iota(jnp.int32, (n_rows, lanes), 0)
    li = lax.broadcasted_iota(jnp.int32, (n_rows, lanes), 1)
    own = wanted(ri, li)

    def update(scores, values):
        m_old = m_ref[...]
        m_new = m_old
        for s in scores:
            m_new = jnp.maximum(m_new, jnp.max(s, axis=1, keepdims=True))
        alpha = jnp.exp(m_old - m_new)
        l = alpha * l_ref[...]
        acc = alpha * acc_ref[...]
        for s, v_rows in zip(scores, values):
            p = jnp.exp(s - m_new)
            l = l + jnp.sum(p, axis=1, keepdims=True)
            acc = acc + (p * v_rows if s.shape[1] == 1 else _dot(p, v_rows, False))
        l_ref[...] = l
        acc_ref[...] = acc
        m_ref[...] = m_new

    scores = []
    for g in range(g_pages):
        if mode == "diff":
            half = n_rows // 2
            s = jnp.concatenate(
                [_dot_nt(q[c * half:(c + 1) * half], k_refs[g][0, pl.ds(c, lanes, stride=2), :], False)
                 for c in range(2)], axis=0)
        else:
            s = _dot_nt(q, k_refs[g][0], False)
        s = jnp.where(own, s * scale, NEG_BIG)
        if mode == "dsa":
            s = s + bias_ref[0, pl.ds(j * g_pages + g, 1), :]
        scores.append(s)
    update(scores, [v_refs[g][0] for g in range(g_pages)])

    @pl.when(j == n_steps - 1)
    def _():
        s_new = jnp.sum(q * knew_ref[0], axis=1, keepdims=True) * scale
        if mode == "dsa":
            s_new = s_new + bias_new_ref[0]
        update([s_new], [vnew_ref[0]])
        o = acc_ref[...] / l_ref[...]
        if mode == "diff":
            half = n_rows // 2
            o = _subln(o[:half] - _lam_scalar(lam_ref, lam_init) * o[half:], g_ref[...], lam_init)
        o_ref[0] = o


def _paged_attn(page_table, q, cache_k, cache_v, k_new, v_new, *, mode, pages_per_step,
                bias=None, bias_new=None, lam_vecs=None, subln_g=None, lam_init=0.0):
    bs, n_rows, d = q.shape
    n_pages = page_table.shape[1]
    g = pages_per_step
    assert n_pages % g == 0
    grid = (bs, n_pages // g)
    e = cache_v.shape[2]
    in_specs = [pl.BlockSpec((1, n_rows, d), lambda b, j, pt: (b, 0, 0))]
    args = [q]
    if mode == "dsa":
        in_specs += [pl.BlockSpec((1, n_pages, bias.shape[2]), lambda b, j, pt: (b, 0, 0)),
                     pl.BlockSpec((1, 1, 1), lambda b, j, pt: (b, 0, 0))]
        args += [bias, bias_new]
        out_rows = n_rows
    else:
        in_specs += [pl.BlockSpec((4, A_HEAD_DIM), lambda b, j, pt: (0, 0)),
                     pl.BlockSpec((1, e), lambda b, j, pt: (0, 0))]
        args += [lam_vecs, subln_g.reshape(1, -1)]
        out_rows = n_rows // 2

    def page_spec(arr, gi):
        return pl.BlockSpec((1,) + arr.shape[1:], lambda b, j, pt: (pt[b, j * g + gi], 0, 0))

    in_specs += [page_spec(cache_k, gi) for gi in range(g)] + [page_spec(cache_v, gi) for gi in range(g)]
    args += [cache_k] * g + [cache_v] * g
    in_specs += [pl.BlockSpec((1, n_rows, d), lambda b, j, pt: (b, 0, 0)),
                 pl.BlockSpec((1, n_rows, e), lambda b, j, pt: (b, 0, 0))]
    args += [k_new, v_new]
    kern = functools.partial(_paged_attn_kernel, pages_per_step=g, scale=d ** -0.5, mode=mode,
                             lam_init=lam_init)
    grid_spec = pltpu.PrefetchScalarGridSpec(
        num_scalar_prefetch=1, grid=grid, in_specs=in_specs,
        out_specs=pl.BlockSpec((1, out_rows, e), lambda b, j, pt: (b, 0, 0)),
        scratch_shapes=[pltpu.VMEM((n_rows, 1), F32), pltpu.VMEM((n_rows, 1), F32),
                        pltpu.VMEM((n_rows, e), F32)])
    return pl.pallas_call(kern, out_shape=jax.ShapeDtypeStruct((bs, out_rows, e), F32),
                          grid_spec=grid_spec, compiler_params=_params("arbitrary", "arbitrary"),
                          name="paged_attn_" + mode)(page_table, *args)


INT32_MIN = -2 ** 31
KEY_OF_NEG_INF = (-8388608) ^ 0x7FFFFFFF


def _topk_bias_into(bias_ref, scores, k_sel):
    r, n = scores.shape
    scores = jnp.where(scores == 0.0, 0.0, scores)
    bits = pltpu.bitcast(scores, jnp.int32)
    key = jnp.where(bits < 0, bits ^ 0x7FFFFFFF, bits)
    kf = float(k_sel)

    def step(t, ans):
        cand = ans + jnp.left_shift(jnp.int32(1), 31 - t)
        cnt = jnp.sum(jnp.where(key >= cand, 1.0, 0.0), axis=1, keepdims=True)
        return jnp.where(cnt >= kf, cand, ans)

    thr = lax.fori_loop(0, 32, step, jnp.full((r, 1), INT32_MIN, jnp.int32))
    gt = key > thr
    eq = key == thr
    cnt_gt = jnp.sum(jnp.where(gt, 1.0, 0.0), axis=1, keepdims=True)
    cnt_eq = jnp.sum(jnp.where(eq, 1.0, 0.0), axis=1, keepdims=True)
    bias_ref[...] = jnp.where(key >= thr, 0.0, NEG_BIG)
    tie = jnp.max(jnp.where((cnt_gt + cnt_eq > kf) & (thr > KEY_OF_NEG_INF), 1.0, 0.0))

    @pl.when(tie > 0.0)
    def _():
        need = kf - cnt_gt
        ri = lax.broadcasted_iota(jnp.int32, (LANES, LANES), 0)
        ci = lax.broadcasted_iota(jnp.int32, (LANES, LANES), 1)
        tri = jnp.where(ri < ci, 1.0, 0.0).astype(BF16)
        run = jnp.zeros((r, 1), F32)
        for c in range(n // LANES):
            sl = slice(c * LANES, (c + 1) * LANES)
            eq_c = jnp.where(eq[:, sl], 1.0, 0.0)
            before = jnp.dot(eq_c.astype(BF16), tri, preferred_element_type=F32) + run
            take = gt[:, sl] | (eq[:, sl] & (before < need))
            bias_ref[:, sl] = jnp.where(take, 0.0, NEG_BIG)
            run = run + jnp.sum(eq_c, axis=1, keepdims=True)


def _dsa_prompt_kernel(qi_ref, kibd_ref, wi_ref, q_ref, k_ref, v_ref, o_ref, bias_ref, *, tq, k_sel,
                       width_step):
    i = pl.program_id(1)
    t = k_ref.shape[1]

    def branch(n):
        wi = wi_ref[0]
        qi = qi_ref[0]
        score = jnp.zeros((tq, n), F32)
        for hp in range(IDX_HEADS // 2):
            q_pair = qi[:, hp * LANES:(hp + 1) * LANES]
            for c in range(2):
                d = _dot_nt(q_pair, kibd_ref[0, c * t:c * t + n, :], False)
                score = score + wi[:, 2 * hp + c:2 * hp + c + 1] * jnp.maximum(d, 0.0)
        row = i * tq + lax.broadcasted_iota(jnp.int32, (tq, n), 0)
        col = lax.broadcasted_iota(jnp.int32, (tq, n), 1)
        causal = col <= row
        bias_view = bias_ref.at[:, pl.ds(0, n)]
        _topk_bias_into(bias_view, jnp.where(causal, score, -jnp.inf), k_sel)
        bias = jnp.where(causal, bias_view[...], NEG_BIG)
        bias4 = jnp.concatenate([bias] * B_GROUP, axis=0)
        q = q_ref[0]
        outs = []
        for kv in range(B_KV_HEADS):
            qg = jnp.concatenate([q[:, (kv * B_GROUP + r) * B_HEAD_DIM:(kv * B_GROUP + r + 1) * B_HEAD_DIM]
                                  for r in range(B_GROUP)], axis=0)
            kg = k_ref[0, 0:n, kv * B_HEAD_DIM:(kv + 1) * B_HEAD_DIM]
            vg = v_ref[0, 0:n, kv * B_HEAD_DIM:(kv + 1) * B_HEAD_DIM]
            s = _dot_nt(qg, kg, False) + bias4
            p = jnp.exp2(s - jnp.max(s, axis=1, keepdims=True))
            l = jnp.sum(p, axis=1, keepdims=True)
            og = _dot(p, vg, False) / l
            outs += [og[r * tq:(r + 1) * tq] for r in range(B_GROUP)]
        o_ref[0] = jnp.concatenate(outs, axis=1).astype(o_ref.dtype)

    for cls in range(t // width_step):
        pl.when((i * tq + tq - 1) // width_step == cls)(functools.partial(branch, (cls + 1) * width_step))


def _dsa_prompt(qi, ki, wi, q, k, v, *, tq):
    b, t, _ = q.shape
    k_sel = min(IDX_TOPK_MAX, t // 4)
    zeros = jnp.zeros_like(ki)
    kibd = jnp.concatenate([jnp.concatenate([ki, zeros], axis=-1),
                            jnp.concatenate([zeros, ki], axis=-1)], axis=1)
    kvw = B_KV_HEADS * B_HEAD_DIM
    kern = functools.partial(_dsa_prompt_kernel, tq=tq, k_sel=k_sel, width_step=min(DSA_WIDTH_STEP, t))
    return pl.pallas_call(
        kern, out_shape=jax.ShapeDtypeStruct((b, t, D_MODEL), BF16), grid=(b, t // tq),
        in_specs=[pl.BlockSpec((1, tq, IDX_HEADS * IDX_DIM), lambda bi, i: (bi, i, 0)),
                  pl.BlockSpec((1, 2 * t, LANES), lambda bi, i: (bi, 0, 0)),
                  pl.BlockSpec((1, tq, IDX_HEADS), lambda bi, i: (bi, i, 0)),
                  pl.BlockSpec((1, tq, D_MODEL), lambda bi, i: (bi, i, 0)),
                  pl.BlockSpec((1, t, kvw), lambda bi, i: (bi, 0, 0)),
                  pl.BlockSpec((1, t, kvw), lambda bi, i: (bi, 0, 0))],
        out_specs=pl.BlockSpec((1, tq, D_MODEL), lambda bi, i: (bi, i, 0)),
        scratch_shapes=[pltpu.VMEM((tq, t), F32)],
        compiler_params=_params("arbitrary", "arbitrary"), name="dsa_prompt",
    )(qi, kibd, wi, q, k, v)


def _idx_scores_kernel(pt_ref, qi_ref, w_ref, *refs, pages_per_step):
    page_refs, o_ref = refs[:pages_per_step], refs[pages_per_step]
    qi = qi_ref[0]
    w = w_ref[0]
    rows = []
    for g in range(pages_per_step):
        d = _dot_nt(qi, page_refs[g][0], True)
        rows.append(jnp.sum(w * jnp.maximum(d, 0.0), axis=0, keepdims=True))
    o_ref[0] = jnp.concatenate(rows, axis=0)


def _idx_scores_paged(page_table, qi, wi, cache_kidx, *, pages_per_step):
    bs = qi.shape[0]
    n_pages = page_table.shape[1]
    g = pages_per_step

    def page_spec(gi):
        return pl.BlockSpec((1, PAGE_SIZE, IDX_DIM), lambda b, j, pt: (pt[b, j * g + gi], 0, 0))

    grid_spec = pltpu.PrefetchScalarGridSpec(
        num_scalar_prefetch=1, grid=(bs, n_pages // g),
        in_specs=[pl.BlockSpec((1, IDX_HEADS, IDX_DIM), lambda b, j, pt: (b, 0, 0)),
                  pl.BlockSpec((1, IDX_HEADS, 1), lambda b, j, pt: (b, 0, 0))]
        + [page_spec(gi) for gi in range(g)],
        out_specs=pl.BlockSpec((1, g, PAGE_SIZE), lambda b, j, pt: (b, j, 0)))
    kern = functools.partial(_idx_scores_kernel, pages_per_step=g)
    return pl.pallas_call(kern, out_shape=jax.ShapeDtypeStruct((bs, n_pages, PAGE_SIZE), F32),
                          grid_spec=grid_spec, compiler_params=_params("arbitrary", "arbitrary"),
                          name="dsa_idx_scores")(page_table, qi, wi, *([cache_kidx] * g))


def _topk_bias_kernel(s_ref, o_ref, *, k_sel):
    _topk_bias_into(o_ref, s_ref[...], k_sel)


def _topk_bias(scores, k_sel):
    return pl.pallas_call(functools.partial(_topk_bias_kernel, k_sel=k_sel),
                          out_shape=jax.ShapeDtypeStruct(scores.shape, F32),
                          compiler_params=pltpu.CompilerParams(vmem_limit_bytes=VMEM_LIMIT_BYTES),
                          name="topk_bias")(scores)


def _pool_prompt_kernel(x_ref, w_ref, sc_ref, o_ref, pad_ref, *, window, rows):
    t = x_ref.shape[1]
    halo = 2 * SUBLANES
    pad_ref[0:halo, :] = jnp.zeros((halo, pad_ref.shape[1]), F32)
    pad_ref[halo:, :] = x_ref[0]
    w = w_ref[0]
    sc = sc_ref[...]

    def body(c, carry):
        r0 = pl.multiple_of(c * rows, rows)
        xh = pad_ref[pl.ds(r0, rows + halo), :]
        x = xh[halo:, :]
        win = x
        for jj in range(1, window):
            win = win + xh[halo - jj:halo - jj + rows, :]
        pos = r0 + lax.broadcasted_iota(jnp.int32, (rows, 1), 0)
        count = jnp.minimum(window, pos + 1).astype(F32)
        mixed = win / count - x
        o_ref[0, pl.ds(r0, rows), :] = _dot(mixed, w, False) * sc
        return carry

    lax.fori_loop(0, t // rows, body, 0)


def _pool_prompt(x, w_group_bf16, scale):
    b, t, d = x.shape
    gd = POOL_GROUP_DIM
    outs = []
    for g, window in enumerate(POOL_WINDOWS):
        kern = functools.partial(_pool_prompt_kernel, window=window, rows=256)
        outs.append(pl.pallas_call(
            kern, out_shape=jax.ShapeDtypeStruct((b, t, gd), F32), grid=(b,),
            in_specs=[pl.BlockSpec((1, t, gd), lambda bi, g=g: (bi, 0, g)),
                      pl.BlockSpec((1, gd, gd), lambda bi, g=g: (g, 0, 0)),
                      pl.BlockSpec((1, gd), lambda bi, g=g: (0, g))],
            out_specs=pl.BlockSpec((1, t, gd), lambda bi: (bi, 0, 0)),
            scratch_shapes=[pltpu.VMEM((t + 2 * SUBLANES, gd), F32)],
            compiler_params=_params("arbitrary"), name="pool_prompt_w%d" % window,
        )(x, w_group_bf16, scale.reshape(1, d)))
    return jnp.concatenate(outs, axis=-1)


def _pool_sample_kernel(x_ref, w_ref, sc_ref, o_ref):
    g = pl.program_id(0)
    window = jnp.left_shift(jnp.int32(2), g)
    x = x_ref[...]
    n = x.shape[1]
    rowi = lax.broadcasted_iota(jnp.int32, x.shape, 1)
    win = jnp.sum(jnp.where(rowi >= n - window, x, 0.0), axis=1)
    mixed = win / window.astype(F32) - x[:, n - 1, :]
    o_ref[...] = _dot(mixed, w_ref[0], True) * sc_ref[...]


def _pool_sample(x_ext, w_group, scale):
    b, n, d = x_ext.shape
    gd = POOL_GROUP_DIM
    return pl.pallas_call(
        _pool_sample_kernel, out_shape=jax.ShapeDtypeStruct((b, d), F32), grid=(POOL_GROUPS,),
        in_specs=[pl.BlockSpec((b, n, gd), lambda g: (0, 0, g)),
                  pl.BlockSpec((1, gd, gd), lambda g: (g, 0, 0)),
                  pl.BlockSpec((1, gd), lambda g: (0, g))],
        out_specs=pl.BlockSpec((b, gd), lambda g: (0, g)),
        compiler_params=_params("arbitrary"), name="pool_sample",
    )(x_ext, w_group, scale.reshape(1, d))


def _router_kernel(x_ref, rw_ref, rb_ref, g_ref):
    tm = x_ref.shape[0]
    logits = _dot(x_ref[...], rw_ref[...], True)
    lane = lax.broadcasted_iota(jnp.int32, logits.shape, 1)
    logits = jnp.where(lane < N_EXPERTS, logits, NEG_BIG)
    e = jnp.exp(logits - jnp.max(logits, axis=1, keepdims=True))
    aff_t = (e / jnp.sum(e, axis=1, keepdims=True)).T
    bias = rb_ref[...]
    aff = [aff_t[x:x + 1, :] for x in range(N_EXPERTS)]
    sel = [aff[x] + bias[x:x + 1, :] for x in range(N_EXPERTS)]
    gsz = EXPERTS_PER_GROUP
    best_score = best_group = None
    for g in range(N_EXPERT_GROUPS):
        a, b, c, d = sel[g * gsz:(g + 1) * gsz]
        hi1, lo1, hi2, lo2 = jnp.maximum(a, b), jnp.minimum(a, b), jnp.maximum(c, d), jnp.minimum(c, d)
        score = jnp.maximum(hi1, hi2) + jnp.maximum(jnp.minimum(hi1, hi2), jnp.maximum(lo1, lo2))
        if g == 0:
            best_score, best_group = score, jnp.zeros(score.shape, jnp.int32)
        else:
            better = score > best_score
            best_group = jnp.where(better, g, best_group)
            best_score = jnp.where(better, score, best_score)
    chosen = []
    for x in range(N_EXPERTS):
        g = x // gsz
        rank = jnp.zeros((1, tm), F32)
        for y in range(g * gsz, (g + 1) * gsz):
            if y == x:
                continue
            ahead = (sel[y] > sel[x]) | ((sel[y] == sel[x]) & (y < x))
            rank = rank + jnp.where(ahead, 1.0, 0.0)
        chosen.append((best_group == g) & (rank < 2.0))
    top_sum = jnp.zeros((1, tm), F32)
    for x in range(N_EXPERTS):
        top_sum = top_sum + jnp.where(chosen[x], aff[x], 0.0)
    rows = [jnp.where(chosen[x], aff[x] / top_sum, 0.0) for x in range(N_EXPERTS)]
    rows += [jnp.where(chosen[x], 1.0, 0.0) for x in range(N_EXPERTS)]
    rows.append(jnp.zeros((LANES - 2 * N_EXPERTS, tm), F32))
    g_ref[...] = jnp.concatenate(rows, axis=0).T


def _router(x, router_w, router_bias, *, tm):
    m, d = x.shape
    tm = min(tm, m)
    rw = jnp.pad(router_w, ((0, 0), (0, LANES - N_EXPERTS)))
    return pl.pallas_call(
        _router_kernel, out_shape=jax.ShapeDtypeStruct((m, LANES), F32), grid=(m // tm,),
        in_specs=[pl.BlockSpec((tm, d), lambda i: (i, 0)),
                  pl.BlockSpec((d, LANES), lambda i: (0, 0)),
                  pl.BlockSpec((N_EXPERTS, 1), lambda i: (0, 0))],
        out_specs=pl.BlockSpec((tm, LANES), lambda i: (i, 0)),
        compiler_params=_params("arbitrary"), name="router",
    )(x, rw, router_bias.reshape(N_EXPERTS, 1))


def _moe_dense_kernel(ids_ref, na_ref, x_ref, g_ref, win_ref, wdn_ref, res_ref, lg_ref, lb_ref, *refs,
                      precise):
    out_refs, acc_ref = refs[:-1], refs[-1]
    e = pl.program_id(1)

    @pl.when(e == 0)
    def _():
        acc_ref[...] = jnp.zeros(acc_ref.shape, F32)

    @pl.when(e < na_ref[0])
    def _():
        h = _dot(x_ref[...], win_ref[0], precise)
        hg, hu = h[:, :D_EXPERT], h[:, D_EXPERT:]
        gates = g_ref[...]
        lane = lax.broadcasted_iota(jnp.int32, gates.shape, 1)
        gate = jnp.sum(jnp.where(lane == ids_ref[e], gates, 0.0), axis=1, keepdims=True)
        a = hg * jax.nn.sigmoid(hg) * hu * gate
        acc_ref[...] += _dot(a, wdn_ref[0], precise)

    @pl.when(e == pl.num_programs(1) - 1)
    def _():
        y = _layer_norm_rows(DEEPNORM_ALPHA * res_ref[...] + acc_ref[...], lg_ref[...], lb_ref[...])
        for o in out_refs:
            o[...] = y.astype(o.dtype)


def _moe_dense(x, routed, w_in, w_down, res, ln_g, ln_b, *, tm, precise, out_dtypes):
    m, d = x.shape
    tm = min(tm, m)
    active = jnp.any(routed[:, N_EXPERTS:2 * N_EXPERTS] > 0.5, axis=0)
    order = jnp.argsort(jnp.logical_not(active), stable=True).astype(jnp.int32)
    n_act = jnp.sum(active.astype(jnp.int32))
    ids = jnp.where(jnp.arange(N_EXPERTS) < n_act, order, order[jnp.maximum(n_act - 1, 0)])
    kern = functools.partial(_moe_dense_kernel, precise=precise)
    grid_spec = pltpu.PrefetchScalarGridSpec(
        num_scalar_prefetch=2, grid=(m // tm, N_EXPERTS),
        in_specs=[pl.BlockSpec((tm, d), lambda i, e, ids, na: (i, 0)),
                  pl.BlockSpec((tm, LANES), lambda i, e, ids, na: (i, 0)),
                  pl.BlockSpec((1, d, 2 * D_EXPERT), lambda i, e, ids, na: (ids[e], 0, 0)),
                  pl.BlockSpec((1, D_EXPERT, d), lambda i, e, ids, na: (ids[e], 0, 0)),
                  pl.BlockSpec((tm, d), lambda i, e, ids, na: (i, 0)),
                  pl.BlockSpec((1, d), lambda i, e, ids, na: (0, 0)),
                  pl.BlockSpec((1, d), lambda i, e, ids, na: (0, 0))],
        out_specs=[pl.BlockSpec((tm, d), lambda i, e, ids, na: (i, 0)) for _ in out_dtypes],
        scratch_shapes=[pltpu.VMEM((tm, d), F32)])
    return pl.pallas_call(
        kern, out_shape=[jax.ShapeDtypeStruct((m, d), dt) for dt in out_dtypes], grid_spec=grid_spec,
        compiler_params=_params("arbitrary", "arbitrary"), name="moe_dense",
    )(ids, n_act.reshape(1), x, routed, w_in, w_down, res, ln_g.reshape(1, d), ln_b.reshape(1, d))


def _moe_plan(gates_and_mask, tile):
    gates = gates_and_mask[:, :N_EXPERTS]
    chosen = gates_and_mask[:, N_EXPERTS:2 * N_EXPERTS] > 0.5
    m = gates.shape[0]
    c = chosen.astype(jnp.int32)
    rank = jnp.cumsum(c, axis=0) - c
    counts = jnp.sum(c, axis=0)
    padded = ((counts + tile - 1) // tile) * tile
    ends = jnp.cumsum(padded)
    slot = (ends - padded)[None, :] + rank
    slot_a = jnp.min(jnp.where(chosen, slot, jnp.int32(2 ** 30)), axis=1)
    slot_b = jnp.max(jnp.where(chosen, slot, jnp.int32(-1)), axis=1)
    gate_a = jnp.sum(jnp.where(chosen & (slot == slot_a[:, None]), gates, 0.0), axis=1)
    gate_b = jnp.sum(jnp.where(chosen & (slot == slot_b[:, None]), gates, 0.0), axis=1)
    n_tiles = 2 * m // tile + N_EXPERTS
    starts = jnp.arange(n_tiles, dtype=jnp.int32) * tile
    tile_expert = jnp.minimum(jnp.sum((starts[:, None] >= ends[None, :]).astype(jnp.int32), axis=1),
                              N_EXPERTS - 1)
    n_used = (ends[-1] // tile).astype(jnp.int32).reshape(1)
    return (jnp.stack([slot_a, slot_b], axis=1), jnp.stack([gate_a, gate_b], axis=1),
            tile_expert, n_used, n_tiles)


def _row_copy(src, src_row, dst, dst_row, sem):
    return pltpu.make_async_copy(src.at[pl.ds(src_row, 1)], dst.at[pl.ds(dst_row, 1)], sem)


def _moe_dispatch_kernel(slots_ref, x_ref, init_ref, xs_ref, sem):
    del init_ref
    t = x_ref.shape[0]

    def copies(r):
        return [_row_copy(x_ref, r, xs_ref, slots_ref[0, 0, k * t + r], sem) for k in range(2)]

    def start(r, carry):
        for cp in copies(r):
            cp.start()
        return carry

    def wait(r, carry):
        for cp in copies(r):
            cp.wait()
        return carry

    lax.fori_loop(0, t, start, 0)
    lax.fori_loop(0, t, wait, 0)


def _moe_dispatch(x, slots, n_rows, *, tile):
    m, d = x.shape
    nt = m // tile
    slots_t = slots.reshape(nt, tile, 2).transpose(0, 2, 1).reshape(nt, 1, 2 * tile)
    return pl.pallas_call(
        _moe_dispatch_kernel, out_shape=jax.ShapeDtypeStruct((n_rows, d), x.dtype), grid=(nt,),
        in_specs=[pl.BlockSpec((1, 1, 2 * tile), lambda i: (i, 0, 0), memory_space=pltpu.SMEM),
                  pl.BlockSpec((tile, d), lambda i: (i, 0)),
                  pl.BlockSpec(memory_space=pl.ANY)],
        out_specs=pl.BlockSpec(memory_space=pl.ANY),
        scratch_shapes=[pltpu.SemaphoreType.DMA],
        input_output_aliases={2: 0},
        compiler_params=_params("arbitrary"), name="moe_dispatch",
    )(slots_t, x, jnp.zeros((n_rows, d), x.dtype))


def _moe_ffn_kernel(te_ref, nu_ref, xs_ref, win_ref, wdn_ref, y_ref, win_bf, wdn_bf):
    i = pl.program_id(0)

    @pl.when(i < nu_ref[0])
    def _():
        prev = te_ref[jnp.maximum(i - 1, 0)]

        @pl.when((i == 0) | (te_ref[i] != prev))
        def _():
            win_bf[...] = win_ref[0].astype(BF16)
            wdn_bf[...] = wdn_ref[0].astype(BF16)

        h = _dot(xs_ref[...], win_bf[...], False)
        hg, hu = h[:, :D_EXPERT], h[:, D_EXPERT:]
        y_ref[...] = _dot(hg * jax.nn.sigmoid(hg) * hu, wdn_bf[...], False)

    @pl.when(i >= nu_ref[0])
    def _():
        y_ref[...] = jnp.zeros(y_ref.shape, F32)


def _moe_ffn(xs, tile_expert, n_used, w_in, w_down, *, tile):
    n_rows, d = xs.shape
    n_tiles = n_rows // tile
    grid_spec = pltpu.PrefetchScalarGridSpec(
        num_scalar_prefetch=2, grid=(n_tiles,),
        in_specs=[pl.BlockSpec((tile, d), lambda i, te, nu: (jnp.minimum(i, nu[0] - 1), 0)),
                  pl.BlockSpec((1, d, 2 * D_EXPERT), lambda i, te, nu: (te[i], 0, 0)),
                  pl.BlockSpec((1, D_EXPERT, d), lambda i, te, nu: (te[i], 0, 0))],
        out_specs=pl.BlockSpec((tile, d), lambda i, te, nu: (i, 0)),
        scratch_shapes=[pltpu.VMEM((d, 2 * D_EXPERT), BF16), pltpu.VMEM((D_EXPERT, d), BF16)])
    return pl.pallas_call(_moe_ffn_kernel, out_shape=jax.ShapeDtypeStruct((n_rows, d), F32),
                          grid_spec=grid_spec, compiler_params=_params("arbitrary"),
                          name="moe_ffn")(tile_expert, n_used, xs, w_in, w_down)


def _moe_combine_kernel(slots_ref, y_ref, g_ref, res_ref, lg_ref, lb_ref, *refs):
    out_refs, (ya_ref, yb_ref, sem) = refs[:-3], refs[-3:]
    t = res_ref.shape[0]

    def copies(r):
        return [_row_copy(y_ref, slots_ref[0, 0, k * t + r], buf, r, sem)
                for k, buf in enumerate((ya_ref, yb_ref))]

    def start(r, carry):
        for cp in copies(r):
            cp.start()
        return carry

    def wait(r, carry):
        for cp in copies(r):
            cp.wait()
        return carry

    lax.fori_loop(0, t, start, 0)
    lax.fori_loop(0, t, wait, 0)
    g = g_ref[...]
    sub = g[:, 0:1] * ya_ref[...] + g[:, 1:2] * yb_ref[...]
    y = _layer_norm_rows(DEEPNORM_ALPHA * res_ref[...] + sub, lg_ref[...], lb_ref[...])
    for o in out_refs:
        o[...] = y.astype(o.dtype)


def _moe_combine(y, slots, gates2, res, ln_g, ln_b, *, tile, out_dtypes):
    m, d = res.shape
    nt = m // tile
    slots_t = slots.reshape(nt, tile, 2).transpose(0, 2, 1).reshape(nt, 1, 2 * tile)
    row = pl.BlockSpec((tile, d), lambda i: (i, 0))
    vec = pl.BlockSpec((1, d), lambda i: (0, 0))
    return pl.pallas_call(
        _moe_combine_kernel, out_shape=[jax.ShapeDtypeStruct((m, d), dt) for dt in out_dtypes], grid=(nt,),
        in_specs=[pl.BlockSpec((1, 1, 2 * tile), lambda i: (i, 0, 0), memory_space=pltpu.SMEM),
                  pl.BlockSpec(memory_space=pl.ANY),
                  pl.BlockSpec((tile, 2), lambda i: (i, 0)), row, vec, vec],
        out_specs=[row for _ in out_dtypes],
        scratch_shapes=[pltpu.VMEM((tile, d), F32), pltpu.VMEM((tile, d), F32), pltpu.SemaphoreType.DMA],
        compiler_params=_params("arbitrary"), name="moe_combine",
    )(slots_t, y, gates2, res, ln_g.reshape(1, d), ln_b.reshape(1, d))


def _post_norm_kernel(x_ref, sub_ref, g_ref, b_ref, *out_refs):
    y = _layer_norm_rows(DEEPNORM_ALPHA * x_ref[...] + sub_ref[...], g_ref[...], b_ref[...])
    for o in out_refs:
        o[...] = y.astype(o.dtype)


def _post_norm(x, sub, g, b, *, tm, out_dtypes):
    m, d = x.shape
    tm = min(tm, m)
    row = pl.BlockSpec((tm, d), lambda i: (i, 0))
    vec = pl.BlockSpec((1, d), lambda i: (0, 0))
    return pl.pallas_call(
        _post_norm_kernel, out_shape=[jax.ShapeDtypeStruct((m, d), dt) for dt in out_dtypes],
        grid=(m // tm,), in_specs=[row, row, vec, vec], out_specs=[row for _ in out_dtypes],
        compiler_params=_params("arbitrary"), name="post_norm",
    )(x, sub, g.reshape(1, d), b.reshape(1, d))


PROMPT_TM = 1024
PROMPT_LN_TM = 256
MOE_TM = 256
MOE_ROW_TILE = 256
ATTN_TQ = 256
DSA_TQ = 128
DSA_WIDTH_STEP = 512
DIFF_PAGES_PER_STEP = 4
DSA_PAGES_PER_STEP = 8


def _moe_prompt(xf, xb, router_w, router_bias, w_in, w_down, ln_g, ln_b):
    del xb
    routed = _router(xf, router_w, router_bias, tm=PROMPT_TM)
    slots, gates2, tile_expert, n_used, n_tiles = _moe_plan(routed, MOE_TM)
    xs = _moe_dispatch(xf, slots, n_tiles * MOE_TM, tile=MOE_ROW_TILE)
    y = _moe_ffn(xs, tile_expert, n_used, w_in, w_down, tile=MOE_TM)
    return _moe_combine(y, slots, gates2, xf, ln_g, ln_b, tile=MOE_ROW_TILE, out_dtypes=(F32, BF16))


def _moe_sample(xs, router_w, router_bias, w_in, w_down, ln_g, ln_b):
    n = xs.shape[0]
    xpad = jnp.pad(xs, ((0, LANES - n), (0, 0)))
    gates = _router(xpad, router_w, router_bias, tm=LANES)[:n]
    (y,) = _moe_dense(xs, gates, w_in, w_down, xs, ln_g, ln_b, tm=n, precise=True, out_dtypes=(F32,))
    return y


def _diff_layer(xpf, xpb, xs, cache_k, cache_v, page_table, w_qkv, w_o, lq1, lk1, lq2, lk2, subln_g,
                ln_g, ln_b, layer_idx, b, t):
    lam_init = 0.8 - 0.6 * math.exp(-0.3 * layer_idx)
    lam_vecs = jnp.stack([lq1, lk1, lq2, lk2]).astype(F32)
    dqk = 2 * A_HEADS * A_HEAD_DIM
    cos_p, sin_p = _rope_tables(jnp.arange(t, dtype=jnp.int32), A_HEAD_DIM)
    wb = w_qkv.astype(BF16)
    tm = PROMPT_TM
    (qb,) = _proj(xpb, wb, col_start=0, n_cols=dqk, tm=tm, tn=1024, rope=(cos_p, sin_p, A_HEAD_DIM),
                  out_dtypes=(BF16,), out_scale=A_HEAD_DIM ** -0.5 * LOG2_E, name="diff_q")
    kf, kb = _proj(xpb, wb, col_start=dqk, n_cols=dqk, tm=tm, tn=1024, rope=(cos_p, sin_p, A_HEAD_DIM),
                   out_dtypes=(F32, BF16), name="diff_k")
    vf, vb = _proj(xpb, wb, col_start=2 * dqk, n_cols=dqk, tm=tm, tn=1024, out_dtypes=(F32, BF16),
                   name="diff_v")
    o = _diff_attn_prompt(qb.reshape(b, t, dqk), kb.reshape(b, t, dqk), vb.reshape(b, t, dqk),
                          lam_vecs, subln_g, lam_init, tq=ATTN_TQ)
    x1f, x1b = _proj(o.reshape(b * t, dqk), w_o.astype(BF16), tm=PROMPT_LN_TM, tn=D_MODEL,
                     ln=(xpf, ln_g, ln_b), out_dtypes=(F32, BF16), name="diff_wo")
    bs = xs.shape[0]
    pos_s = jnp.full((bs,), PAST_LEN, jnp.int32)
    cos_s, sin_s = _rope_tables(pos_s, A_HEAD_DIM)
    (qs,) = _proj(xs, w_qkv, col_start=0, n_cols=dqk, tm=bs, tn=1024, precise=True,
                  rope=(cos_s, sin_s, A_HEAD_DIM), name="diff_q_s")
    (ks,) = _proj(xs, w_qkv, col_start=dqk, n_cols=dqk, tm=bs, tn=1024, precise=True,
                  rope=(cos_s, sin_s, A_HEAD_DIM), name="diff_k_s")
    (vs,) = _proj(xs, w_qkv, col_start=2 * dqk, n_cols=dqk, tm=bs, tn=1024, precise=True, name="diff_v_s")
    n_pool = cache_k.shape[0]

    def streams_first(a):
        a = a.reshape(bs, A_HEADS, 2, A_HEAD_DIM)
        return a.transpose(0, 2, 1, 3).reshape(bs, 2 * A_HEADS, A_HEAD_DIM)

    vs_rows = vs.reshape(bs, A_HEADS, 2 * A_HEAD_DIM)
    os_ = _paged_attn(page_table, streams_first(qs),
                      cache_k.reshape(n_pool, PAGE_SIZE * 2 * A_HEADS, A_HEAD_DIM),
                      cache_v.reshape(n_pool, PAGE_SIZE * A_HEADS, 2 * A_HEAD_DIM),
                      streams_first(ks), jnp.concatenate([vs_rows, vs_rows], axis=1),
                      mode="diff", pages_per_step=DIFF_PAGES_PER_STEP, lam_vecs=lam_vecs, subln_g=subln_g,
                      lam_init=lam_init)
    (xs1,) = _proj(os_.reshape(bs, dqk), w_o, tm=bs, tn=D_MODEL, precise=True, ln=(xs, ln_g, ln_b),
                   name="diff_wo_s")
    state = (kf.reshape(b, t, 2 * A_HEADS, A_HEAD_DIM), vf.reshape(b, t, A_HEADS, 2 * A_HEAD_DIM),
             ks.reshape(bs, 1, 2 * A_HEADS, A_HEAD_DIM), vs.reshape(bs, 1, A_HEADS, 2 * A_HEAD_DIM))
    return x1f, x1b, xs1, state


def _dsa_layer(xpf, xpb, xs, cache_k, cache_v, cache_kidx, page_table, w_in, w_o, ln_g, ln_b, b, t):
    dq = B_HEADS * B_HEAD_DIM
    dkv = B_KV_HEADS * B_HEAD_DIM
    dqi = IDX_HEADS * IDX_DIM
    c_k, c_v, c_qi, c_tail = dq, dq + dkv, dq + 2 * dkv, dq + 2 * dkv + dqi
    w_tail = jnp.pad(w_in[:, c_tail:], ((0, 0), (0, LANES - (IDX_DIM + IDX_HEADS))))

    def project(x, w, wt, tm, precise, cos_h, sin_h, cos_i, sin_i, tag):
        both = (F32,) if precise else (F32, BF16)
        low = (F32,) if precise else (BF16,)
        kw = dict(tm=tm, precise=precise)
        q = _proj(x, w, col_start=0, n_cols=dq, tn=1024, rope=(cos_h, sin_h, B_HEAD_DIM), out_dtypes=low,
                  out_scale=1.0 if precise else B_HEAD_DIM ** -0.5 * LOG2_E, name="dsa_q" + tag, **kw)
        k = _proj(x, w, col_start=c_k, n_cols=dkv, tn=dkv, rope=(cos_h, sin_h, B_HEAD_DIM),
                  out_dtypes=both, name="dsa_k" + tag, **kw)
        v = _proj(x, w, col_start=c_v, n_cols=dkv, tn=dkv, out_dtypes=both, name="dsa_v" + tag, **kw)
        qi = _proj(x, w, col_start=c_qi, n_cols=dqi, tn=dqi, rope=(cos_i, sin_i, IDX_DIM),
                   out_dtypes=low, name="dsa_qi" + tag, **kw)
        ki = _proj(x, wt, tn=LANES, rope=(cos_i, sin_i, IDX_DIM), out_dtypes=both,
                   name="dsa_ki" + tag, **kw)
        (wi,) = _proj(x, wt, tn=LANES, name="dsa_wi" + tag, **kw)
        return q, k, v, qi, ki, wi[:, IDX_DIM:IDX_DIM + IDX_HEADS]

    pos_p = jnp.arange(t, dtype=jnp.int32)
    q, k, v, qi, ki, wi = project(xpb, w_in.astype(BF16), w_tail.astype(BF16), PROMPT_TM, False,
                                  *_rope_tables(pos_p, B_HEAD_DIM), *_rope_tables(pos_p, IDX_DIM), "")
    o = _dsa_prompt(qi[0].reshape(b, t, dqi), ki[1][:, :IDX_DIM].reshape(b, t, IDX_DIM),
                    wi.reshape(b, t, IDX_HEADS), q[0].reshape(b, t, dq), k[1].reshape(b, t, dkv),
                    v[1].reshape(b, t, dkv), tq=DSA_TQ)
    x1f, x1b = _proj(o.reshape(b * t, dq), w_o.astype(BF16), tm=PROMPT_LN_TM, tn=D_MODEL,
                     ln=(xpf, ln_g, ln_b), out_dtypes=(F32, BF16), name="dsa_wo")
    bs = xs.shape[0]
    pos_s = jnp.full((bs,), PAST_LEN, jnp.int32)
    qs, ks, vs, qis, kis, wis = project(xs, w_in, w_tail, bs, True, *_rope_tables(pos_s, B_HEAD_DIM),
                                        *_rope_tables(pos_s, IDX_DIM), "_s")
    qs, ks, vs, qis, kis = qs[0], ks[0], vs[0], qis[0], kis[0][:, :IDX_DIM]
    qis3 = qis.reshape(bs, IDX_HEADS, IDX_DIM)
    wis3 = wis.reshape(bs, IDX_HEADS, 1)
    n_pages = page_table.shape[1]
    past_scores = _idx_scores_paged(page_table, qis3, wis3, cache_kidx, pages_per_step=DSA_PAGES_PER_STEP)
    new_page = jnp.pad(kis.reshape(bs, 1, IDX_DIM), ((0, 0), (0, PAGE_SIZE - 1), (0, 0)))
    new_scores = _idx_scores_paged(jnp.arange(bs, dtype=jnp.int32).reshape(bs, 1), qis3, wis3, new_page,
                                   pages_per_step=1)
    n_past = n_pages * PAGE_SIZE
    all_scores = jnp.concatenate(
        [past_scores.reshape(bs, n_past), new_scores[:, 0, :1], jnp.full((bs, LANES - 1), -jnp.inf, F32)], axis=1)
    bias_all = _topk_bias(all_scores, min(IDX_TOPK_MAX, (n_past + 1) // 4))
    n_pool = cache_k.shape[0]
    per_head = lambda a: jnp.repeat(a.reshape(bs, B_KV_HEADS, B_HEAD_DIM), B_GROUP, axis=1)
    os_ = _paged_attn(page_table, qs.reshape(bs, B_HEADS, B_HEAD_DIM),
                      cache_k.reshape(n_pool, PAGE_SIZE * B_KV_HEADS, B_HEAD_DIM),
                      cache_v.reshape(n_pool, PAGE_SIZE * B_KV_HEADS, B_HEAD_DIM),
                      per_head(ks), per_head(vs), mode="dsa", pages_per_step=DSA_PAGES_PER_STEP,
                      bias=jnp.repeat(bias_all[:, :n_past], B_KV_HEADS, axis=1).reshape(
                          bs, n_pages, PAGE_SIZE * B_KV_HEADS),
                      bias_new=bias_all[:, n_past:n_past + 1].reshape(bs, 1, 1))
    (xs1,) = _proj(os_.reshape(bs, dq), w_o, tm=bs, tn=D_MODEL, precise=True, ln=(xs, ln_g, ln_b),
                   name="dsa_wo_s")
    state = (k[0].reshape(b, t, B_KV_HEADS, B_HEAD_DIM), v[0].reshape(b, t, B_KV_HEADS, B_HEAD_DIM),
             ki[0][:, :IDX_DIM].reshape(b, t, IDX_DIM),
             ks.reshape(bs, 1, B_KV_HEADS, B_HEAD_DIM), vs.reshape(bs, 1, B_KV_HEADS, B_HEAD_DIM),
             kis.reshape(bs, 1, IDX_DIM))
    return x1f, x1b, xs1, state


def _pool_layer(xpf, xs, state_pool, w_group, scale, ln_g, ln_b, b, t):
    xp3 = xpf.reshape(b, t, D_MODEL)
    sub = _pool_prompt(xp3, w_group.astype(BF16), scale)
    x1f, x1b = _post_norm(xpf, sub.reshape(b * t, D_MODEL), ln_g, ln_b, tm=PROMPT_LN_TM,
                          out_dtypes=(F32, BF16))
    bs = xs.shape[0]
    xs_ext = jnp.concatenate([state_pool.astype(xs.dtype), xs.reshape(bs, 1, D_MODEL)], axis=1)
    sub_s = _pool_sample(xs_ext, w_group, scale)
    (xs1,) = _post_norm(xs, sub_s, ln_g, ln_b, tm=bs, out_dtypes=(F32,))
    return x1f, x1b, xs1, (xp3[:, -POOL_STATE_LEN:], xs_ext[:, -POOL_STATE_LEN:])


def kernel(x_prompt, x_sample, cache_l0_k, cache_l0_v, cache_l1_k, cache_l1_v, cache_l1_kidx, state_l2_pool, cache_l3_k, cache_l3_v, page_table, router_w, router_bias, l0_w_qkv, l0_w_o, l0_lam_q1, l0_lam_k1, l0_lam_q2, l0_lam_k2, l0_subln_g, l0_ln1_g, l0_ln1_b, l0_moe_w_in, l0_moe_w_down, l0_ln2_g, l0_ln2_b, l1_w_in, l1_w_o, l1_ln1_g, l1_ln1_b, l1_moe_w_in, l1_moe_w_down, l1_ln2_g, l1_ln2_b, l2_w_group, l2_scale, l2_ln1_g, l2_ln1_b, l2_moe_w_in, l2_moe_w_down, l2_ln2_g, l2_ln2_b, l3_w_qkv, l3_w_o, l3_lam_q1, l3_lam_k1, l3_lam_q2, l3_lam_k2, l3_subln_g, l3_ln1_g, l3_ln1_b, l3_moe_w_in, l3_moe_w_down, l3_ln2_g, l3_ln2_b):
    b, t, d = x_prompt.shape
    bs = x_sample.shape[0]
    xpf = x_prompt.reshape(b * t, d)
    xpb = xpf.astype(BF16)
    xs = x_sample.reshape(bs, d)
    moe_p = [(l0_moe_w_in, l0_moe_w_down, l0_ln2_g, l0_ln2_b), (l1_moe_w_in, l1_moe_w_down, l1_ln2_g, l1_ln2_b),
             (l2_moe_w_in, l2_moe_w_down, l2_ln2_g, l2_ln2_b), (l3_moe_w_in, l3_moe_w_down, l3_ln2_g, l3_ln2_b)]
    states = []
    for i in range(DEPTH):
        if i == 0:
            xpf, xpb, xs, st = _diff_layer(xpf, xpb, xs, cache_l0_k, cache_l0_v, page_table, l0_w_qkv, l0_w_o,
                                           l0_lam_q1, l0_lam_k1, l0_lam_q2, l0_lam_k2, l0_subln_g,
                                           l0_ln1_g, l0_ln1_b, i, b, t)
        elif i == 1:
            xpf, xpb, xs, st = _dsa_layer(xpf, xpb, xs, cache_l1_k, cache_l1_v, cache_l1_kidx, page_table,
                                          l1_w_in, l1_w_o, l1_ln1_g, l1_ln1_b, b, t)
        elif i == 2:
            xpf, xpb, xs, st = _pool_layer(xpf, xs, state_l2_pool, l2_w_group, l2_scale, l2_ln1_g, l2_ln1_b, b, t)
        else:
            xpf, xpb, xs, st = _diff_layer(xpf, xpb, xs, cache_l3_k, cache_l3_v, page_table, l3_w_qkv, l3_w_o,
                                           l3_lam_q1, l3_lam_k1, l3_lam_q2, l3_lam_k2, l3_subln_g,
                                           l3_ln1_g, l3_ln1_b, i, b, t)
        states.append(st)
        w_in, w_down, ln_g, ln_b = moe_p[i]
        xpf, xpb = _moe_prompt(xpf, xpb, router_w, router_bias, w_in, w_down, ln_g, ln_b)
        xs = _moe_sample(xs, router_w, router_bias, w_in, w_down, ln_g, ln_b)
    l0, l1, l2, l3 = states
    return (xpf.reshape(b, t, d), xs.reshape(bs, 1, d),
            l0[0], l0[1], l0[2], l0[3],
            l1[0], l1[1], l1[2], l1[3], l1[4], l1[5],
            l2[0], l2[1],
            l3[0], l3[1], l3[2], l3[3])
```

```python
import functools
import math

import jax
import jax.numpy as jnp
from jax import lax
from jax.experimental import pallas as pl
from jax.experimental.pallas import tpu as pltpu

F32 = jnp.float32
BF16 = jnp.bfloat16

D_MODEL = 2048
DEPTH = 4
PAST_LEN = 16384
PAGE_SIZE = 128
ROPE_THETA = 10000.0
LN_EPS = 1e-5
A_HEADS = 8
A_HEAD_DIM = D_MODEL // A_HEADS // 2
B_HEADS = 16
B_HEAD_DIM = D_MODEL // B_HEADS
B_KV_HEADS = 4
B_GROUP = B_HEADS // B_KV_HEADS
IDX_HEADS = 16
IDX_DIM = 64
IDX_TOPK_MAX = 256
POOL_WINDOWS = (2, 4, 8, 16)
POOL_GROUPS = 4
POOL_GROUP_DIM = D_MODEL // POOL_GROUPS
POOL_STATE_LEN = max(POOL_WINDOWS) - 1
N_EXPERTS = 16
N_EXPERT_GROUPS = 4
EXPERTS_PER_GROUP = N_EXPERTS // N_EXPERT_GROUPS
D_EXPERT = D_MODEL // 4
DEEPNORM_ALPHA = (2 * DEPTH) ** 0.25

LANES = 128
SUBLANES = 8
VMEM_LIMIT_BYTES = 56 * 1024 * 1024

NEG_BIG = -1e30
HIGHEST = lax.Precision.HIGHEST


def _params(*sem):
    return pltpu.CompilerParams(dimension_semantics=sem, vmem_limit_bytes=VMEM_LIMIT_BYTES)


def _dot(a, b, precise):
    if precise:
        return jnp.dot(a.astype(F32), b.astype(F32), precision=HIGHEST, preferred_element_type=F32)
    return jnp.dot(a.astype(BF16), b.astype(BF16), preferred_element_type=F32)


def _dot_nt(a, b, precise):
    dims = (((1,), (1,)), ((), ()))
    if precise:
        return lax.dot_general(a.astype(F32), b.astype(F32), dims, precision=HIGHEST,
                               preferred_element_type=F32)
    return lax.dot_general(a.astype(BF16), b.astype(BF16), dims, preferred_element_type=F32)


def _layer_norm_rows(r, g, b):
    mu = jnp.mean(r, axis=-1, keepdims=True)
    c = r - mu
    var = jnp.mean(c * c, axis=-1, keepdims=True)
    return c * lax.rsqrt(var + LN_EPS) * g + b


def _rope_tables(pos, head_dim):
    half = head_dim // 2
    inv_freq = ROPE_THETA ** (-jnp.arange(half, dtype=F32) * 2.0 / head_dim)
    ang = pos.astype(F32)[:, None] * inv_freq[None, :]
    cos, sin = jnp.cos(ang), jnp.sin(ang)
    reps = LANES // head_dim
    cos_t = jnp.tile(jnp.concatenate([cos, cos], axis=-1), (1, reps))
    sin_t = jnp.tile(jnp.concatenate([-sin, sin], axis=-1), (1, reps))
    return cos_t, sin_t


def _rope_lanes(x, cos_t, sin_t, head_dim):
    half = head_dim // 2
    if head_dim == LANES:
        partner = pltpu.roll(x, half, axis=1)
    else:
        lane = lax.broadcasted_iota(jnp.int32, x.shape, 1)
        first = (lane % head_dim) < half
        partner = jnp.where(first, pltpu.roll(x, LANES - half, axis=1), pltpu.roll(x, half, axis=1))
    return x * cos_t + partner * sin_t


def _proj_kernel(*refs, precise, rope_dim, has_ln, n_out, out_scale):
    it = iter(refs)
    x_ref, w_ref = next(it), next(it)
    cos_ref = sin_ref = res_ref = g_ref = b_ref = None
    if rope_dim:
        cos_ref, sin_ref = next(it), next(it)
    if has_ln:
        res_ref, g_ref, b_ref = next(it), next(it), next(it)
    outs = [next(it) for _ in range(n_out)]
    acc = _dot(x_ref[...], w_ref[...], precise)
    if rope_dim:
        cos_t, sin_t = cos_ref[...], sin_ref[...]
        pieces = [_rope_lanes(acc[:, c * LANES:(c + 1) * LANES], cos_t, sin_t, rope_dim)
                  for c in range(acc.shape[1] // LANES)]
        acc = jnp.concatenate(pieces, axis=1) if len(pieces) > 1 else pieces[0]
    if has_ln:
        acc = _layer_norm_rows(DEEPNORM_ALPHA * res_ref[...] + acc, g_ref[...], b_ref[...])
    if out_scale != 1.0:
        acc = acc * out_scale
    for o in outs:
        o[...] = acc.astype(o.dtype)


def _proj(x, w, *, col_start=0, n_cols=None, tm, tn, precise=False, rope=None, ln=None,
          out_dtypes=(F32,), out_scale=1.0, name="proj"):
    m, k = x.shape
    tm = min(tm, m)
    n_cols = w.shape[1] - col_start if n_cols is None else n_cols
    assert m % tm == 0 and n_cols % tn == 0 and col_start % tn == 0
    cb = col_start // tn
    grid = (n_cols // tn, m // tm)
    in_specs = [pl.BlockSpec((tm, k), lambda j, i: (i, 0)),
                pl.BlockSpec((k, tn), lambda j, i: (0, cb + j))]
    args = [x, w]
    rope_dim = 0
    if rope is not None:
        cos_t, sin_t, rope_dim = rope
        period = cos_t.shape[0] // tm
        in_specs += [pl.BlockSpec((tm, LANES), lambda j, i: (i % period, 0))] * 2
        args += [cos_t, sin_t]
    if ln is not None:
        assert tn == n_cols
        res, g, b = ln
        in_specs += [pl.BlockSpec((tm, tn), lambda j, i: (i, 0)),
                     pl.BlockSpec((1, tn), lambda j, i: (0, 0)),
                     pl.BlockSpec((1, tn), lambda j, i: (0, 0))]
        args += [res, g.reshape(1, -1), b.reshape(1, -1)]
    out_shape = [jax.ShapeDtypeStruct((m, n_cols), dt) for dt in out_dtypes]
    out_specs = [pl.BlockSpec((tm, tn), lambda j, i: (i, j)) for _ in out_dtypes]
    kern = functools.partial(_proj_kernel, precise=precise, rope_dim=rope_dim, has_ln=ln is not None,
                             n_out=len(out_dtypes), out_scale=out_scale)
    res = pl.pallas_call(kern, out_shape=out_shape, grid=grid, in_specs=in_specs, out_specs=out_specs,
                         compiler_params=_params("arbitrary", "arbitrary"), name=name)(*args)
    return res


def _lam_scalar(lam_ref, lam_init):
    v = lam_ref[...]
    a = jnp.sum(v[0:1] * v[1:2], axis=1, keepdims=True)
    b = jnp.sum(v[2:3] * v[3:4], axis=1, keepdims=True)
    return jnp.exp(a) - jnp.exp(b) + lam_init


def _subln(o, g, lam_init):
    ms = jnp.mean(o * o, axis=-1, keepdims=True)
    return o * lax.rsqrt(ms + LN_EPS) * g * (1.0 - lam_init)


def _diff_attn_kernel(lam_ref, g_ref, q_ref, k_ref, v_ref, o_ref, *, tq, lam_init, n_blocks):
    i = pl.program_id(2)
    hd = A_HEAD_DIM
    lam = _lam_scalar(lam_ref, lam_init)
    row = lax.broadcasted_iota(jnp.int32, (tq, tq), 0)
    col = lax.broadcasted_iota(jnp.int32, (tq, tq), 1)
    on_or_below_diag = col <= row

    def branch(c):
        n_left = c * tq
        q = q_ref[0]
        streams = []
        for s_ in range(2):
            lanes = slice(s_ * hd, (s_ + 1) * hd)
            qs = q[:, lanes]
            sd = _dot_nt(qs, k_ref[0, n_left:n_left + tq, lanes], False)
            sd = jnp.where(on_or_below_diag, sd, NEG_BIG)
            m = jnp.max(sd, axis=1, keepdims=True)
            if c > 0:
                sl = _dot_nt(qs, k_ref[0, 0:n_left, lanes], False)
                m = jnp.maximum(m, jnp.max(sl, axis=1, keepdims=True))
            pd = jnp.exp2(sd - m)
            l = jnp.sum(pd, axis=1, keepdims=True)
            o = _dot(pd, v_ref[0, n_left:n_left + tq, :], False)
            if c > 0:
                p_left = jnp.exp2(sl - m)
                l = l + jnp.sum(p_left, axis=1, keepdims=True)
                o = o + _dot(p_left, v_ref[0, 0:n_left, :], False)
            streams.append(o / l)
        o = streams[0] - lam * streams[1]
        o_ref[0] = _subln(o, g_ref[...], lam_init).astype(o_ref.dtype)

    for c in range(n_blocks):
        pl.when(i == c)(functools.partial(branch, c))


LOG2_E = 1.4426950408889634


def _diff_attn_prompt(q, k, v, lam_vecs, subln_g, lam_init, *, tq):
    b, t, _ = q.shape
    w = 2 * A_HEAD_DIM
    grid = (b, A_HEADS, t // tq)
    kern = functools.partial(_diff_attn_kernel, tq=tq, lam_init=lam_init, n_blocks=t // tq)
    return pl.pallas_call(
        kern, out_shape=jax.ShapeDtypeStruct((b, t, A_HEADS * w), BF16), grid=grid,
        in_specs=[pl.BlockSpec((4, A_HEAD_DIM), lambda bi, h, i: (0, 0)),
                  pl.BlockSpec((1, w), lambda bi, h, i: (0, 0)),
                  pl.BlockSpec((1, tq, w), lambda bi, h, i: (bi, i, h)),
                  pl.BlockSpec((1, t, w), lambda bi, h, i: (bi, 0, h)),
                  pl.BlockSpec((1, t, w), lambda bi, h, i: (bi, 0, h))],
        out_specs=pl.BlockSpec((1, tq, w), lambda bi, h, i: (bi, i, h)),
        compiler_params=_params("arbitrary", "arbitrary", "arbitrary"), name="diff_attn_prompt",
    )(lam_vecs, subln_g.reshape(1, w), q, k, v)


def _paged_attn_kernel(pt_ref, *refs, pages_per_step, scale, mode, lam_init):
    g_pages = pages_per_step
    it = iter(refs)
    q_ref = next(it)
    bias_ref = bias_new_ref = lam_ref = g_ref = None
    if mode == "dsa":
        bias_ref, bias_new_ref = next(it), next(it)
    else:
        lam_ref, g_ref = next(it), next(it)
    k_refs = [next(it) for _ in range(g_pages)]
    v_refs = [next(it) for _ in range(g_pages)]
    knew_ref, vnew_ref = next(it), next(it)
    o_ref = next(it)
    m_ref, l_ref, acc_ref = next(it), next(it), next(it)
    j = pl.program_id(1)
    n_steps = pl.num_programs(1)

    @pl.when(j == 0)
    def _():
        m_ref[...] = jnp.full(m_ref.shape, NEG_BIG, F32)
        l_ref[...] = jnp.zeros(l_ref.shape, F32)
        acc_ref[...] = jnp.zeros(acc_ref.shape, F32)

    q = q_ref[0]
    n_rows = q.shape[0]
    if mode == "diff":
        lanes = PAGE_SIZE * A_HEADS
        wanted = lambda r, lane: (lane % A_HEADS) == (r % A_HEADS)
    else:
        lanes = PAGE_SIZE * B_KV_HEADS
        wanted = lambda r, lane: (lane % B_KV_HEADS) == (r // B_GROUP)
    ri = lax.broadcasted_iota(jnp.int32, (n_rows, lanes), 0)
    li = lax.broadcasted_iota(jnp.int32, (n_rows, lanes), 1)
    own = wanted(ri, li)

    def update(scores, values):
        m_old = m_ref[...]
        m_new = m_old
        for s in scores:
            m_new = jnp.maximum(m_new, jnp.max(s, axis=1, keepdims=True))
        alpha = jnp.exp(m_old - m_new)
        l = alpha * l_ref[...]
        acc = alpha * acc_ref[...]
        for s, v_rows in zip(scores, values):
            p = jnp.exp(s - m_new)
            l = l + jnp.sum(p, axis=1, keepdims=True)
            acc = acc + (p * v_rows if s.shape[1] == 1 else _dot(p, v_rows, False))
        l_ref[...] = l
        acc_ref[...] = acc
        m_ref[...] = m_new

    scores = []
    for g in range(g_pages):
        if mode == "diff":
            half = n_rows // 2
            s = jnp.concatenate(
                [_dot_nt(q[c * half:(c + 1) * half], k_refs[g][0, pl.ds(c, lanes, stride=2), :], False)
                 for c in range(2)], axis=0)
        else:
            s = _dot_nt(q, k_refs[g][0], False)
        s = jnp.where(own, s * scale, NEG_BIG)
        if mode == "dsa":
            s = s + bias_ref[0, pl.ds(j * g_pages + g, 1), :]
        scores.append(s)
    update(scores, [v_refs[g][0] for g in range(g_pages)])

    @pl.when(j == n_steps - 1)
    def _():
        s_new = jnp.sum(q * knew_ref[0], axis=1, keepdims=True) * scale
        if mode == "dsa":
            s_new = s_new + bias_new_ref[0]
        update([s_new], [vnew_ref[0]])
        o = acc_ref[...] / l_ref[...]
        if mode == "diff":
            half = n_rows // 2
            o = _subln(o[:half] - _lam_scalar(lam_ref, lam_init) * o[half:], g_ref[...], lam_init)
        o_ref[0] = o


def _paged_attn(page_table, q, cache_k, cache_v, k_new, v_new, *, mode, pages_per_step,
                bias=None, bias_new=None, lam_vecs=None, subln_g=None, lam_init=0.0):
    bs, n_rows, d = q.shape
    n_pages = page_table.shape[1]
    g = pages_per_step
    assert n_pages % g == 0
    grid = (bs, n_pages // g)
    e = cache_v.shape[2]
    in_specs = [pl.BlockSpec((1, n_rows, d), lambda b, j, pt: (b, 0, 0))]
    args = [q]
    if mode == "dsa":
        in_specs += [pl.BlockSpec((1, n_pages, bias.shape[2]), lambda b, j, pt: (b, 0, 0)),
                     pl.BlockSpec((1, 1, 1), lambda b, j, pt: (b, 0, 0))]
        args += [bias, bias_new]
        out_rows = n_rows
    else:
        in_specs += [pl.BlockSpec((4, A_HEAD_DIM), lambda b, j, pt: (0, 0)),
                     pl.BlockSpec((1, e), lambda b, j, pt: (0, 0))]
        args += [lam_vecs, subln_g.reshape(1, -1)]
        out_rows = n_rows // 2

    def page_spec(arr, gi):
        return pl.BlockSpec((1,) + arr.shape[1:], lambda b, j, pt: (pt[b, j * g + gi], 0, 0))

    in_specs += [page_spec(cache_k, gi) for gi in range(g)] + [page_spec(cache_v, gi) for gi in range(g)]
    args += [cache_k] * g + [cache_v] * g
    in_specs += [pl.BlockSpec((1, n_rows, d), lambda b, j, pt: (b, 0, 0)),
                 pl.BlockSpec((1, n_rows, e), lambda b, j, pt: (b, 0, 0))]
    args += [k_new, v_new]
    kern = functools.partial(_paged_attn_kernel, pages_per_step=g, scale=d ** -0.5, mode=mode,
                             lam_init=lam_init)
    grid_spec = pltpu.PrefetchScalarGridSpec(
        num_scalar_prefetch=1, grid=grid, in_specs=in_specs,
        out_specs=pl.BlockSpec((1, out_rows, e), lambda b, j, pt: (b, 0, 0)),
        scratch_shapes=[pltpu.VMEM((n_rows, 1), F32), pltpu.VMEM((n_rows, 1), F32),
                        pltpu.VMEM((n_rows, e), F32)])
    return pl.pallas_call(kern, out_shape=jax.ShapeDtypeStruct((bs, out_rows, e), F32),
                          grid_spec=grid_spec, compiler_params=_params("arbitrary", "arbitrary"),
                          name="paged_attn_" + mode)(page_table, *args)


INT32_MIN = -2 ** 31
KEY_OF_NEG_INF = (-8388608) ^ 0x7FFFFFFF


def _topk_bias_into(bias_ref, scores, k_sel):
    r, n = scores.shape
    scores = jnp.where(scores == 0.0, 0.0, scores)
    bits = pltpu.bitcast(scores, jnp.int32)
    key = jnp.where(bits < 0, bits ^ 0x7FFFFFFF, bits)
    kf = float(k_sel)

    def step(t, ans):
        cand = ans + jnp.left_shift(jnp.int32(1), 31 - t)
        cnt = jnp.sum(jnp.where(key >= cand, 1.0, 0.0), axis=1, keepdims=True)
        return jnp.where(cnt >= kf, cand, ans)

    thr = lax.fori_loop(0, 32, step, jnp.full((r, 1), INT32_MIN, jnp.int32), unroll=4)
    gt = key > thr
    eq = key == thr
    cnt_gt = jnp.sum(jnp.where(gt, 1.0, 0.0), axis=1, keepdims=True)
    cnt_eq = jnp.sum(jnp.where(eq, 1.0, 0.0), axis=1, keepdims=True)
    bias_ref[...] = jnp.where(key >= thr, 0.0, NEG_BIG)
    tie = jnp.max(jnp.where((cnt_gt + cnt_eq > kf) & (thr > KEY_OF_NEG_INF), 1.0, 0.0))

    @pl.when(tie > 0.0)
    def _():
        need = kf - cnt_gt
        ri = lax.broadcasted_iota(jnp.int32, (LANES, LANES), 0)
        ci = lax.broadcasted_iota(jnp.int32, (LANES, LANES), 1)
        tri = jnp.where(ri < ci, 1.0, 0.0).astype(BF16)
        run = jnp.zeros((r, 1), F32)
        for c in range(n // LANES):
            sl = slice(c * LANES, (c + 1) * LANES)
            eq_c = jnp.where(eq[:, sl], 1.0, 0.0)
            before = jnp.dot(eq_c.astype(BF16), tri, preferred_element_type=F32) + run
            take = gt[:, sl] | (eq[:, sl] & (before < need))
            bias_ref[:, sl] = jnp.where(take, 0.0, NEG_BIG)
            run = run + jnp.sum(eq_c, axis=1, keepdims=True)


def _dsa_prompt_kernel(qi_ref, kibd_ref, wi_ref, q_ref, k_ref, v_ref, o_ref, bias_ref, *, tq, k_sel,
                       width_step):
    i = pl.program_id(1)
    t = k_ref.shape[1]

    def branch(n):
        wi = wi_ref[0]
        qi = qi_ref[0]
        score = jnp.zeros((tq, n), F32)
        for hp in range(IDX_HEADS // 2):
            q_pair = qi[:, hp * LANES:(hp + 1) * LANES]
            for c in range(2):
                d = _dot_nt(q_pair, kibd_ref[0, c * t:c * t + n, :], False)
                score = score + wi[:, 2 * hp + c:2 * hp + c + 1] * jnp.maximum(d, 0.0)
        row = i * tq + lax.broadcasted_iota(jnp.int32, (tq, n), 0)
        col = lax.broadcasted_iota(jnp.int32, (tq, n), 1)
        causal = col <= row
        bias_view = bias_ref.at[:, pl.ds(0, n)]
        _topk_bias_into(bias_view, jnp.where(causal, score, -jnp.inf), k_sel)
        bias = jnp.where(causal, bias_view[...], NEG_BIG)
        bias4 = jnp.concatenate([bias] * B_GROUP, axis=0)
        q = q_ref[0]
        outs = []
        for kv in range(B_KV_HEADS):
            qg = jnp.concatenate([q[:, (kv * B_GROUP + r) * B_HEAD_DIM:(kv * B_GROUP + r + 1) * B_HEAD_DIM]
                                  for r in range(B_GROUP)], axis=0)
            kg = k_ref[0, 0:n, kv * B_HEAD_DIM:(kv + 1) * B_HEAD_DIM]
            vg = v_ref[0, 0:n, kv * B_HEAD_DIM:(kv + 1) * B_HEAD_DIM]
            s = _dot_nt(qg, kg, False) + bias4
            p = jnp.exp2(s - jnp.max(s, axis=1, keepdims=True))
            l = jnp.sum(p, axis=1, keepdims=True)
            og = _dot(p, vg, False) / l
            outs += [og[r * tq:(r + 1) * tq] for r in range(B_GROUP)]
        o_ref[0] = jnp.concatenate(outs, axis=1).astype(o_ref.dtype)

    for cls in range(t // width_step):
        pl.when((i * tq + tq - 1) // width_step == cls)(functools.partial(branch, (cls + 1) * width_step))


def _dsa_prompt(qi, ki, wi, q, k, v, *, tq):
    b, t, _ = q.shape
    k_sel = min(IDX_TOPK_MAX, t // 4)
    zeros = jnp.zeros_like(ki)
    kibd = jnp.concatenate([jnp.concatenate([ki, zeros], axis=-1),
                            jnp.concatenate([zeros, ki], axis=-1)], axis=1)
    kvw = B_KV_HEADS * B_HEAD_DIM
    kern = functools.partial(_dsa_prompt_kernel, tq=tq, k_sel=k_sel, width_step=min(DSA_WIDTH_STEP, t))
    return pl.pallas_call(
        kern, out_shape=jax.ShapeDtypeStruct((b, t, D_MODEL), BF16), grid=(b, t // tq),
        in_specs=[pl.BlockSpec((1, tq, IDX_HEADS * IDX_DIM), lambda bi, i: (bi, i, 0)),
                  pl.BlockSpec((1, 2 * t, LANES), lambda bi, i: (bi, 0, 0)),
                  pl.BlockSpec((1, tq, IDX_HEADS), lambda bi, i: (bi, i, 0)),
                  pl.BlockSpec((1, tq, D_MODEL), lambda bi, i: (bi, i, 0)),
                  pl.BlockSpec((1, t, kvw), lambda bi, i: (bi, 0, 0)),
                  pl.BlockSpec((1, t, kvw), lambda bi, i: (bi, 0, 0))],
        out_specs=pl.BlockSpec((1, tq, D_MODEL), lambda bi, i: (bi, i, 0)),
        scratch_shapes=[pltpu.VMEM((tq, t), F32)],
        compiler_params=_params("arbitrary", "arbitrary"), name="dsa_prompt",
    )(qi, kibd, wi, q, k, v)


def _idx_scores_kernel(pt_ref, qi_ref, w_ref, *refs, pages_per_step):
    page_refs, o_ref = refs[:pages_per_step], refs[pages_per_step]
    qi = qi_ref[0]
    w = w_ref[0]
    rows = []
    for g in range(pages_per_step):
        d = _dot(qi, page_refs[g][0], True)
        rows.append(jnp.sum(w * jnp.maximum(d, 0.0), axis=0, keepdims=True))
    o_ref[0] = jnp.concatenate(rows, axis=0)


def _idx_scores_paged(page_table, qi, wi, cache_kidx, *, pages_per_step):
    bs = qi.shape[0]
    n_pages = page_table.shape[1]
    g = pages_per_step

    def page_spec(gi):
        return pl.BlockSpec((1, IDX_DIM, PAGE_SIZE), lambda b, j, pt: (pt[b, j * g + gi], 0, 0))

    grid_spec = pltpu.PrefetchScalarGridSpec(
        num_scalar_prefetch=1, grid=(bs, n_pages // g),
        in_specs=[pl.BlockSpec((1, IDX_HEADS, IDX_DIM), lambda b, j, pt: (b, 0, 0)),
                  pl.BlockSpec((1, IDX_HEADS, 1), lambda b, j, pt: (b, 0, 0))]
        + [page_spec(gi) for gi in range(g)],
        out_specs=pl.BlockSpec((1, g, PAGE_SIZE), lambda b, j, pt: (b, j, 0)))
    kern = functools.partial(_idx_scores_kernel, pages_per_step=g)
    return pl.pallas_call(kern, out_shape=jax.ShapeDtypeStruct((bs, n_pages, PAGE_SIZE), F32),
                          grid_spec=grid_spec, compiler_params=_params("arbitrary", "arbitrary"),
                          name="dsa_idx_scores")(page_table, qi, wi, *([cache_kidx] * g))


def _topk_bias_kernel(s_ref, o_ref, *, k_sel):
    _topk_bias_into(o_ref, s_ref[...], k_sel)


def _topk_bias(scores, k_sel):
    return pl.pallas_call(functools.partial(_topk_bias_kernel, k_sel=k_sel),
                          out_shape=jax.ShapeDtypeStruct(scores.shape, F32),
                          compiler_params=pltpu.CompilerParams(vmem_limit_bytes=VMEM_LIMIT_BYTES),
                          name="topk_bias")(scores)


def _pool_prompt_kernel(x_ref, w_ref, sc_ref, o_ref, pad_ref, *, window, rows):
    t = x_ref.shape[1]
    halo = 2 * SUBLANES
    pad_ref[0:halo, :] = jnp.zeros((halo, pad_ref.shape[1]), F32)
    pad_ref[halo:, :] = x_ref[0]
    w = w_ref[0]
    sc = sc_ref[...]

    def body(c, carry):
        r0 = pl.multiple_of(c * rows, rows)
        xh = pad_ref[pl.ds(r0, rows + halo), :]
        x = xh[halo:, :]
        win = x
        for jj in range(1, window):
            win = win + xh[halo - jj:halo - jj + rows, :]
        pos = r0 + lax.broadcasted_iota(jnp.int32, (rows, 1), 0)
        count = jnp.minimum(window, pos + 1).astype(F32)
        mixed = win / count - x
        o_ref[0, pl.ds(r0, rows), :] = _dot(mixed, w, False) * sc
        return carry

    lax.fori_loop(0, t // rows, body, 0)


def _pool_prompt(x, w_group_bf16, scale):
    b, t, d = x.shape
    gd = POOL_GROUP_DIM
    outs = []
    for g, window in enumerate(POOL_WINDOWS):
        kern = functools.partial(_pool_prompt_kernel, window=window, rows=256)
        outs.append(pl.pallas_call(
            kern, out_shape=jax.ShapeDtypeStruct((b, t, gd), F32), grid=(b,),
            in_specs=[pl.BlockSpec((1, t, gd), lambda bi, g=g: (bi, 0, g)),
                      pl.BlockSpec((1, gd, gd), lambda bi, g=g: (g, 0, 0)),
                      pl.BlockSpec((1, gd), lambda bi, g=g: (0, g))],
            out_specs=pl.BlockSpec((1, t, gd), lambda bi: (bi, 0, 0)),
            scratch_shapes=[pltpu.VMEM((t + 2 * SUBLANES, gd), F32)],
            compiler_params=_params("arbitrary"), name="pool_prompt_w%d" % window,
        )(x, w_group_bf16, scale.reshape(1, d)))
    return jnp.concatenate(outs, axis=-1)


def _pool_sample_kernel(x_ref, w_ref, sc_ref, o_ref):
    g = pl.program_id(0)
    window = jnp.left_shift(jnp.int32(2), g)
    x = x_ref[...]
    n = x.shape[1]
    rowi = lax.broadcasted_iota(jnp.int32, x.shape, 1)
    win = jnp.sum(jnp.where(rowi >= n - window, x, 0.0), axis=1)
    mixed = win / window.astype(F32) - x[:, n - 1, :]
    o_ref[...] = _dot(mixed, w_ref[0], True) * sc_ref[...]


def _pool_sample(x_ext, w_group, scale):
    b, n, d = x_ext.shape
    gd = POOL_GROUP_DIM
    return pl.pallas_call(
        _pool_sample_kernel, out_shape=jax.ShapeDtypeStruct((b, d), F32), grid=(POOL_GROUPS,),
        in_specs=[pl.BlockSpec((b, n, gd), lambda g: (0, 0, g)),
                  pl.BlockSpec((1, gd, gd), lambda g: (g, 0, 0)),
                  pl.BlockSpec((1, gd), lambda g: (0, g))],
        out_specs=pl.BlockSpec((b, gd), lambda g: (0, g)),
        compiler_params=_params("arbitrary"), name="pool_sample",
    )(x_ext, w_group, scale.reshape(1, d))


def _split_bf16(a):
    hi = a.astype(BF16)
    return hi, (a - hi.astype(F32)).astype(BF16)


def _router_kernel(x_ref, rw_ref, rb_ref, g_ref, *, full_f32):
    tm = x_ref.shape[0]
    if full_f32:
        logits = _dot(x_ref[...], rw_ref[...], True)
    else:
        x_hi, x_lo = _split_bf16(x_ref[...])
        w_hi, w_lo = _split_bf16(rw_ref[...])
        logits = (jnp.dot(x_hi, w_hi, preferred_element_type=F32)
                  + jnp.dot(x_lo, w_hi, preferred_element_type=F32)
                  + jnp.dot(x_hi, w_lo, preferred_element_type=F32))
    lane = lax.broadcasted_iota(jnp.int32, logits.shape, 1)
    logits = jnp.where(lane < N_EXPERTS, logits, NEG_BIG)
    e = jnp.exp(logits - jnp.max(logits, axis=1, keepdims=True))
    aff_t = (e / jnp.sum(e, axis=1, keepdims=True)).T
    bias = rb_ref[...]
    aff = [aff_t[x:x + 1, :] for x in range(N_EXPERTS)]
    sel = [aff[x] + bias[x:x + 1, :] for x in range(N_EXPERTS)]
    gsz = EXPERTS_PER_GROUP
    best_score = best_group = None
    for g in range(N_EXPERT_GROUPS):
        a, b, c, d = sel[g * gsz:(g + 1) * gsz]
        hi1, lo1, hi2, lo2 = jnp.maximum(a, b), jnp.minimum(a, b), jnp.maximum(c, d), jnp.minimum(c, d)
        score = jnp.maximum(hi1, hi2) + jnp.maximum(jnp.minimum(hi1, hi2), jnp.maximum(lo1, lo2))
        if g == 0:
            best_score, best_group = score, jnp.zeros(score.shape, jnp.int32)
        else:
            better = score > best_score
            best_group = jnp.where(better, g, best_group)
            best_score = jnp.where(better, score, best_score)
    chosen = []
    for x in range(N_EXPERTS):
        g = x // gsz
        rank = jnp.zeros((1, tm), F32)
        for y in range(g * gsz, (g + 1) * gsz):
            if y == x:
                continue
            ahead = (sel[y] > sel[x]) | ((sel[y] == sel[x]) & (y < x))
            rank = rank + jnp.where(ahead, 1.0, 0.0)
        chosen.append((best_group == g) & (rank < 2.0))
    top_sum = jnp.zeros((1, tm), F32)
    for x in range(N_EXPERTS):
        top_sum = top_sum + jnp.where(chosen[x], aff[x], 0.0)
    rows = [jnp.where(chosen[x], aff[x] / top_sum, 0.0) for x in range(N_EXPERTS)]
    rows += [jnp.where(chosen[x], 1.0, 0.0) for x in range(N_EXPERTS)]
    rows.append(jnp.zeros((LANES - 2 * N_EXPERTS, tm), F32))
    g_ref[...] = jnp.concatenate(rows, axis=0).T


def _router(x, router_w, router_bias, *, tm, full_f32):
    m, d = x.shape
    tm = min(tm, m)
    rw = jnp.pad(router_w, ((0, 0), (0, LANES - N_EXPERTS)))
    return pl.pallas_call(
        functools.partial(_router_kernel, full_f32=full_f32),
        out_shape=jax.ShapeDtypeStruct((m, LANES), F32), grid=(m // tm,),
        in_specs=[pl.BlockSpec((tm, d), lambda i: (i, 0)),
                  pl.BlockSpec((d, LANES), lambda i: (0, 0)),
                  pl.BlockSpec((N_EXPERTS, 1), lambda i: (0, 0))],
        out_specs=pl.BlockSpec((tm, LANES), lambda i: (i, 0)),
        compiler_params=_params("arbitrary"), name="router",
    )(x, rw, router_bias.reshape(N_EXPERTS, 1))


def _moe_dense_kernel(ids_ref, na_ref, x_ref, g_ref, win_ref, wdn_ref, res_ref, lg_ref, lb_ref, *refs,
                      precise):
    out_refs, acc_ref = refs[:-1], refs[-1]
    e = pl.program_id(1)

    @pl.when(e == 0)
    def _():
        acc_ref[...] = jnp.zeros(acc_ref.shape, F32)

    @pl.when(e < na_ref[0])
    def _():
        h = _dot(x_ref[...], win_ref[0], precise)
        hg, hu = h[:, :D_EXPERT], h[:, D_EXPERT:]
        gates = g_ref[...]
        lane = lax.broadcasted_iota(jnp.int32, gates.shape, 1)
        gate = jnp.sum(jnp.where(lane == ids_ref[e], gates, 0.0), axis=1, keepdims=True)
        a = hg * jax.nn.sigmoid(hg) * hu * gate
        acc_ref[...] += _dot(a, wdn_ref[0], precise)

    @pl.when(e == pl.num_programs(1) - 1)
    def _():
        y = _layer_norm_rows(DEEPNORM_ALPHA * res_ref[...] + acc_ref[...], lg_ref[...], lb_ref[...])
        for o in out_refs:
            o[...] = y.astype(o.dtype)


def _moe_dense(x, routed, w_in, w_down, res, ln_g, ln_b, *, tm, precise, out_dtypes):
    m, d = x.shape
    tm = min(tm, m)
    active = jnp.any(routed[:, N_EXPERTS:2 * N_EXPERTS] > 0.5, axis=0)
    order = jnp.argsort(jnp.logical_not(active), stable=True).astype(jnp.int32)
    n_act = jnp.sum(active.astype(jnp.int32))
    ids = jnp.where(jnp.arange(N_EXPERTS) < n_act, order, order[jnp.maximum(n_act - 1, 0)])
    kern = functools.partial(_moe_dense_kernel, precise=precise)
    grid_spec = pltpu.PrefetchScalarGridSpec(
        num_scalar_prefetch=2, grid=(m // tm, N_EXPERTS),
        in_specs=[pl.BlockSpec((tm, d), lambda i, e, ids, na: (i, 0)),
                  pl.BlockSpec((tm, LANES), lambda i, e, ids, na: (i, 0)),
                  pl.BlockSpec((1, d, 2 * D_EXPERT), lambda i, e, ids, na: (ids[e], 0, 0)),
                  pl.BlockSpec((1, D_EXPERT, d), lambda i, e, ids, na: (ids[e], 0, 0)),
                  pl.BlockSpec((tm, d), lambda i, e, ids, na: (i, 0)),
                  pl.BlockSpec((1, d), lambda i, e, ids, na: (0, 0)),
                  pl.BlockSpec((1, d), lambda i, e, ids, na: (0, 0))],
        out_specs=[pl.BlockSpec((tm, d), lambda i, e, ids, na: (i, 0)) for _ in out_dtypes],
        scratch_shapes=[pltpu.VMEM((tm, d), F32)])
    return pl.pallas_call(
        kern, out_shape=[jax.ShapeDtypeStruct((m, d), dt) for dt in out_dtypes], grid_spec=grid_spec,
        compiler_params=_params("arbitrary", "arbitrary"), name="moe_dense",
    )(ids, n_act.reshape(1), x, routed, w_in, w_down, res, ln_g.reshape(1, d), ln_b.reshape(1, d))


def _moe_plan(gates_and_mask, tile):
    gates = gates_and_mask[:, :N_EXPERTS]
    chosen = gates_and_mask[:, N_EXPERTS:2 * N_EXPERTS] > 0.5
    m = gates.shape[0]
    c = chosen.astype(jnp.int32)
    rank = jnp.cumsum(c, axis=0) - c
    counts = jnp.sum(c, axis=0)
    padded = ((counts + tile - 1) // tile) * tile
    ends = jnp.cumsum(padded)
    slot = (ends - padded)[None, :] + rank
    slot_a = jnp.min(jnp.where(chosen, slot, jnp.int32(2 ** 30)), axis=1)
    slot_b = jnp.max(jnp.where(chosen, slot, jnp.int32(-1)), axis=1)
    gate_a = jnp.sum(jnp.where(chosen & (slot == slot_a[:, None]), gates, 0.0), axis=1)
    gate_b = jnp.sum(jnp.where(chosen & (slot == slot_b[:, None]), gates, 0.0), axis=1)
    n_tiles = 2 * m // tile + N_EXPERTS
    starts = jnp.arange(n_tiles, dtype=jnp.int32) * tile
    tile_expert = jnp.minimum(jnp.sum((starts[:, None] >= ends[None, :]).astype(jnp.int32), axis=1),
                              N_EXPERTS - 1)
    n_used = (ends[-1] // tile).astype(jnp.int32).reshape(1)
    return (jnp.stack([slot_a, slot_b], axis=1), jnp.stack([gate_a, gate_b], axis=1),
            tile_expert, n_used, n_tiles, ends.astype(jnp.int32), padded.astype(jnp.int32))


def _row_copy(src, src_row, dst, dst_row, sem):
    return pltpu.make_async_copy(src.at[pl.ds(src_row, 1)], dst.at[pl.ds(dst_row, 1)], sem)


def _moe_dispatch_kernel(ends_ref, padded_ref, slots_ref, x_ref, xs_ref, zero_ref, zero_sem, sem, *,
                         expert_tile):
    i = pl.program_id(0)
    t = slots_ref.shape[2] // 2

    @pl.when(i == 0)
    def _():
        zero_ref[...] = jnp.zeros(zero_ref.shape, zero_ref.dtype)

        def clear_tile(first_row):
            first = pl.multiple_of(first_row, expert_tile)
            return pltpu.make_async_copy(zero_ref, xs_ref.at[pl.ds(first, expert_tile)], zero_sem)

        def clear_all(act):
            for e in range(N_EXPERTS):
                pl.when(padded_ref[e] > 0)(lambda e=e: act(clear_tile(ends_ref[e] - expert_tile)))
            for j in range(xs_ref.shape[0] // expert_tile):
                pl.when(j * expert_tile >= ends_ref[N_EXPERTS - 1])(
                    lambda j=j: act(clear_tile(j * expert_tile)))

        clear_all(lambda cp: cp.start())
        clear_all(lambda cp: cp.wait())

    def start(r, carry):
        for k in range(2):
            _row_copy(x_ref, i * t + r, xs_ref, slots_ref[0, 0, k * t + r], sem).start()
        return carry

    lax.fori_loop(0, t, start, 0, unroll=4)

    def drain_one_step():
        rows = xs_ref.at[pl.ds(0, 2 * t)]
        pltpu.make_async_copy(rows, rows, sem).wait()

    pl.when(i > 0)(drain_one_step)
    pl.when(i == pl.num_programs(0) - 1)(drain_one_step)


def _moe_dispatch(x, slots, ends, padded, n_rows, *, tile, expert_tile):
    m, d = x.shape
    nt = m // tile
    slots_t = slots.reshape(nt, tile, 2).transpose(0, 2, 1).reshape(nt, 1, 2 * tile)
    grid_spec = pltpu.PrefetchScalarGridSpec(
        num_scalar_prefetch=2, grid=(nt,),
        in_specs=[pl.BlockSpec((1, 1, 2 * tile), lambda i, en, pa: (i, 0, 0), memory_space=pltpu.SMEM),
                  pl.BlockSpec(memory_space=pl.ANY)],
        out_specs=pl.BlockSpec(memory_space=pl.ANY),
        scratch_shapes=[pltpu.VMEM((expert_tile, d), x.dtype), pltpu.SemaphoreType.DMA,
                        pltpu.SemaphoreType.DMA])
    kern = functools.partial(_moe_dispatch_kernel, expert_tile=expert_tile)
    return pl.pallas_call(kern, out_shape=jax.ShapeDtypeStruct((n_rows, d), x.dtype), grid_spec=grid_spec,
                          compiler_params=_params("arbitrary"), name="moe_dispatch")(ends, padded, slots_t, x)


def _moe_ffn_kernel(te_ref, nu_ref, xs_ref, win_ref, wdn_ref, y_ref, win_bf, wdn_bf):
    i = pl.program_id(0)

    @pl.when(i < nu_ref[0])
    def _():
        prev = te_ref[jnp.maximum(i - 1, 0)]

        @pl.when((i == 0) | (te_ref[i] != prev))
        def _():
            win_bf[...] = win_ref[0].astype(BF16)
            wdn_bf[...] = wdn_ref[0].astype(BF16)

        h = _dot(xs_ref[...], win_bf[...], False)
        hg, hu = h[:, :D_EXPERT], h[:, D_EXPERT:]
        y_ref[...] = _dot(hg * jax.nn.sigmoid(hg) * hu, wdn_bf[...], False)

    @pl.when(i >= nu_ref[0])
    def _():
        y_ref[...] = jnp.zeros(y_ref.shape, F32)


def _moe_ffn(xs, tile_expert, n_used, w_in, w_down, *, tile):
    n_rows, d = xs.shape
    n_tiles = n_rows // tile
    grid_spec = pltpu.PrefetchScalarGridSpec(
        num_scalar_prefetch=2, grid=(n_tiles,),
        in_specs=[pl.BlockSpec((tile, d), lambda i, te, nu: (jnp.minimum(i, nu[0] - 1), 0)),
                  pl.BlockSpec((1, d, 2 * D_EXPERT), lambda i, te, nu: (te[i], 0, 0)),
                  pl.BlockSpec((1, D_EXPERT, d), lambda i, te, nu: (te[i], 0, 0))],
        out_specs=pl.BlockSpec((tile, d), lambda i, te, nu: (i, 0)),
        scratch_shapes=[pltpu.VMEM((d, 2 * D_EXPERT), BF16), pltpu.VMEM((D_EXPERT, d), BF16)])
    return pl.pallas_call(_moe_ffn_kernel, out_shape=jax.ShapeDtypeStruct((n_rows, d), F32),
                          grid_spec=grid_spec, compiler_params=_params("arbitrary"),
                          name="moe_ffn")(tile_expert, n_used, xs, w_in, w_down)


def _moe_combine_kernel(slots_ref, next_slots_ref, y_ref, g_ref, res_ref, lg_ref, lb_ref, *refs):
    out_refs, (ya_ref, yb_ref, sems) = refs[:-3], refs[-3:]
    i = pl.program_id(0)
    t = res_ref.shape[0]
    cur = lax.rem(i, 2)

    def gather(s_ref, buf):
        def start(r, carry):
            _row_copy(y_ref, s_ref[0, 0, r], ya_ref.at[buf], r, sems.at[buf]).start()
            _row_copy(y_ref, s_ref[0, 0, t + r], yb_ref.at[buf], r, sems.at[buf]).start()
            return carry

        lax.fori_loop(0, t, start, 0, unroll=4)

    pl.when(i == 0)(lambda: gather(slots_ref, 0))
    pl.when(i + 1 < pl.num_programs(0))(lambda: gather(next_slots_ref, 1 - cur))
    for buf_ref in (ya_ref, yb_ref):
        pltpu.make_async_copy(y_ref.at[pl.ds(0, t)], buf_ref.at[cur], sems.at[cur]).wait()
    g = g_ref[...]
    sub = g[:, 0:1] * ya_ref[cur] + g[:, 1:2] * yb_ref[cur]
    y = _layer_norm_rows(DEEPNORM_ALPHA * res_ref[...] + sub, lg_ref[...], lb_ref[...])
    for o in out_refs:
        o[...] = y.astype(o.dtype)


def _moe_combine(y, slots, gates2, res, ln_g, ln_b, *, tile, out_dtypes):
    m, d = res.shape
    nt = m // tile
    slots_t = slots.reshape(nt, tile, 2).transpose(0, 2, 1).reshape(nt, 1, 2 * tile)
    row = pl.BlockSpec((tile, d), lambda i: (i, 0))
    vec = pl.BlockSpec((1, d), lambda i: (0, 0))
    return pl.pallas_call(
        _moe_combine_kernel, out_shape=[jax.ShapeDtypeStruct((m, d), dt) for dt in out_dtypes], grid=(nt,),
        in_specs=[pl.BlockSpec((1, 1, 2 * tile), lambda i: (i, 0, 0), memory_space=pltpu.SMEM),
                  pl.BlockSpec((1, 1, 2 * tile), lambda i: (jnp.minimum(i + 1, nt - 1), 0, 0),
                               memory_space=pltpu.SMEM),
                  pl.BlockSpec(memory_space=pl.ANY),
                  pl.BlockSpec((tile, 2), lambda i: (i, 0)), row, vec, vec],
        out_specs=[row for _ in out_dtypes],
        scratch_shapes=[pltpu.VMEM((2, tile, d), F32), pltpu.VMEM((2, tile, d), F32),
                        pltpu.SemaphoreType.DMA((2,))],
        compiler_params=_params("arbitrary"), name="moe_combine",
    )(slots_t, slots_t, y, gates2, res, ln_g.reshape(1, d), ln_b.reshape(1, d))


def _post_norm_kernel(x_ref, sub_ref, g_ref, b_ref, *out_refs):
    y = _layer_norm_rows(DEEPNORM_ALPHA * x_ref[...] + sub_ref[...], g_ref[...], b_ref[...])
    for o in out_refs:
        o[...] = y.astype(o.dtype)


def _post_norm(x, sub, g, b, *, tm, out_dtypes):
    m, d = x.shape
    tm = min(tm, m)
    row = pl.BlockSpec((tm, d), lambda i: (i, 0))
    vec = pl.BlockSpec((1, d), lambda i: (0, 0))
    return pl.pallas_call(
        _post_norm_kernel, out_shape=[jax.ShapeDtypeStruct((m, d), dt) for dt in out_dtypes],
        grid=(m // tm,), in_specs=[row, row, vec, vec], out_specs=[row for _ in out_dtypes],
        compiler_params=_params("arbitrary"), name="post_norm",
    )(x, sub, g.reshape(1, d), b.reshape(1, d))


PROMPT_TM = 1024
PROMPT_LN_TM = 256
MOE_TM = 512
MOE_ROW_TILE = 256
ATTN_TQ = 256
DSA_TQ = 128
DSA_WIDTH_STEP = 512
DIFF_PAGES_PER_STEP = 8
DSA_PAGES_PER_STEP = 8


def _moe_prompt(xf, xb, router_w, router_bias, w_in, w_down, ln_g, ln_b):
    del xb
    routed = _router(xf, router_w, router_bias, tm=PROMPT_TM, full_f32=False)
    slots, gates2, tile_expert, n_used, n_tiles, ends, padded = _moe_plan(routed, MOE_TM)
    xs = _moe_dispatch(xf, slots, ends, padded, n_tiles * MOE_TM, tile=MOE_ROW_TILE, expert_tile=MOE_TM)
    y = _moe_ffn(xs, tile_expert, n_used, w_in, w_down, tile=MOE_TM)
    return _moe_combine(y, slots, gates2, xf, ln_g, ln_b, tile=MOE_ROW_TILE, out_dtypes=(F32, BF16))


def _moe_sample(xs, router_w, router_bias, w_in, w_down, ln_g, ln_b):
    n = xs.shape[0]
    xpad = jnp.pad(xs, ((0, LANES - n), (0, 0)))
    gates = _router(xpad, router_w, router_bias, tm=LANES, full_f32=True)[:n]
    (y,) = _moe_dense(xs, gates, w_in, w_down, xs, ln_g, ln_b, tm=n, precise=True, out_dtypes=(F32,))
    return y


def _diff_layer(xpf, xpb, xs, cache_k, cache_v, page_table, w_qkv, w_o, lq1, lk1, lq2, lk2, subln_g,
                ln_g, ln_b, layer_idx, b, t):
    lam_init = 0.8 - 0.6 * math.exp(-0.3 * layer_idx)
    lam_vecs = jnp.stack([lq1, lk1, lq2, lk2]).astype(F32)
    dqk = 2 * A_HEADS * A_HEAD_DIM
    cos_p, sin_p = _rope_tables(jnp.arange(t, dtype=jnp.int32), A_HEAD_DIM)
    wb = w_qkv.astype(BF16)
    tm = PROMPT_TM
    (qb,) = _proj(xpb, wb, col_start=0, n_cols=dqk, tm=tm, tn=1024, rope=(cos_p, sin_p, A_HEAD_DIM),
                  out_dtypes=(BF16,), out_scale=A_HEAD_DIM ** -0.5 * LOG2_E, name="diff_q")
    kf, kb = _proj(xpb, wb, col_start=dqk, n_cols=dqk, tm=tm, tn=1024, rope=(cos_p, sin_p, A_HEAD_DIM),
                   out_dtypes=(F32, BF16), name="diff_k")
    vf, vb = _proj(xpb, wb, col_start=2 * dqk, n_cols=dqk, tm=tm, tn=1024, out_dtypes=(F32, BF16),
                   name="diff_v")
    o = _diff_attn_prompt(qb.reshape(b, t, dqk), kb.reshape(b, t, dqk), vb.reshape(b, t, dqk),
                          lam_vecs, subln_g, lam_init, tq=ATTN_TQ)
    x1f, x1b = _proj(o.reshape(b * t, dqk), w_o.astype(BF16), tm=PROMPT_LN_TM, tn=D_MODEL,
                     ln=(xpf, ln_g, ln_b), out_dtypes=(F32, BF16), name="diff_wo")
    bs = xs.shape[0]
    pos_s = jnp.full((bs,), PAST_LEN, jnp.int32)
    cos_s, sin_s = _rope_tables(pos_s, A_HEAD_DIM)
    (qs,) = _proj(xs, w_qkv, col_start=0, n_cols=dqk, tm=bs, tn=1024, precise=True,
                  rope=(cos_s, sin_s, A_HEAD_DIM), name="diff_q_s")
    (ks,) = _proj(xs, w_qkv, col_start=dqk, n_cols=dqk, tm=bs, tn=1024, precise=True,
                  rope=(cos_s, sin_s, A_HEAD_DIM), name="diff_k_s")
    (vs,) = _proj(xs, w_qkv, col_start=2 * dqk, n_cols=dqk, tm=bs, tn=1024, precise=True, name="diff_v_s")
    n_pool = cache_k.shape[0]

    def streams_first(a):
        a = a.reshape(bs, A_HEADS, 2, A_HEAD_DIM)
        return a.transpose(0, 2, 1, 3).reshape(bs, 2 * A_HEADS, A_HEAD_DIM)

    vs_rows = vs.reshape(bs, A_HEADS, 2 * A_HEAD_DIM)
    os_ = _paged_attn(page_table, streams_first(qs),
                      cache_k.reshape(n_pool, PAGE_SIZE * 2 * A_HEADS, A_HEAD_DIM),
                      cache_v.reshape(n_pool, PAGE_SIZE * A_HEADS, 2 * A_HEAD_DIM),
                      streams_first(ks), jnp.concatenate([vs_rows, vs_rows], axis=1),
                      mode="diff", pages_per_step=DIFF_PAGES_PER_STEP, lam_vecs=lam_vecs, subln_g=subln_g,
                      lam_init=lam_init)
    (xs1,) = _proj(os_.reshape(bs, dqk), w_o, tm=bs, tn=D_MODEL, precise=True, ln=(xs, ln_g, ln_b),
                   name="diff_wo_s")
    state = (kf.reshape(b, t, 2 * A_HEADS, A_HEAD_DIM), vf.reshape(b, t, A_HEADS, 2 * A_HEAD_DIM),
             ks.reshape(bs, 1, 2 * A_HEADS, A_HEAD_DIM), vs.reshape(bs, 1, A_HEADS, 2 * A_HEAD_DIM))
    return x1f, x1b, xs1, state


def _dsa_layer(xpf, xpb, xs, cache_k, cache_v, cache_kidx, page_table, w_in, w_o, ln_g, ln_b, b, t):
    dq = B_HEADS * B_HEAD_DIM
    dkv = B_KV_HEADS * B_HEAD_DIM
    dqi = IDX_HEADS * IDX_DIM
    c_k, c_v, c_qi, c_tail = dq, dq + dkv, dq + 2 * dkv, dq + 2 * dkv + dqi
    w_tail = jnp.pad(w_in[:, c_tail:], ((0, 0), (0, LANES - (IDX_DIM + IDX_HEADS))))

    def project(x, w, wt, tm, precise, cos_h, sin_h, cos_i, sin_i, tag):
        both = (F32,) if precise else (F32, BF16)
        low = (F32,) if precise else (BF16,)
        kw = dict(tm=tm, precise=precise)
        q = _proj(x, w, col_start=0, n_cols=dq, tn=1024, rope=(cos_h, sin_h, B_HEAD_DIM), out_dtypes=low,
                  out_scale=1.0 if precise else B_HEAD_DIM ** -0.5 * LOG2_E, name="dsa_q" + tag, **kw)
        k = _proj(x, w, col_start=c_k, n_cols=dkv, tn=dkv, rope=(cos_h, sin_h, B_HEAD_DIM),
                  out_dtypes=both, name="dsa_k" + tag, **kw)
        v = _proj(x, w, col_start=c_v, n_cols=dkv, tn=dkv, out_dtypes=both, name="dsa_v" + tag, **kw)
        qi = _proj(x, w, col_start=c_qi, n_cols=dqi, tn=dqi, rope=(cos_i, sin_i, IDX_DIM),
                   out_dtypes=low, name="dsa_qi" + tag, **kw)
        ki = _proj(x, wt, tn=LANES, rope=(cos_i, sin_i, IDX_DIM), out_dtypes=both,
                   name="dsa_ki" + tag, **kw)
        (wi,) = _proj(x, wt, tn=LANES, name="dsa_wi" + tag, **kw)
        return q, k, v, qi, ki, wi[:, IDX_DIM:IDX_DIM + IDX_HEADS]

    pos_p = jnp.arange(t, dtype=jnp.int32)
    q, k, v, qi, ki, wi = project(xpb, w_in.astype(BF16), w_tail.astype(BF16), PROMPT_TM, False,
                                  *_rope_tables(pos_p, B_HEAD_DIM), *_rope_tables(pos_p, IDX_DIM), "")
    o = _dsa_prompt(qi[0].reshape(b, t, dqi), ki[1][:, :IDX_DIM].reshape(b, t, IDX_DIM),
                    wi.reshape(b, t, IDX_HEADS), q[0].reshape(b, t, dq), k[1].reshape(b, t, dkv),
                    v[1].reshape(b, t, dkv), tq=DSA_TQ)
    x1f, x1b = _proj(o.reshape(b * t, dq), w_o.astype(BF16), tm=PROMPT_LN_TM, tn=D_MODEL,
                     ln=(xpf, ln_g, ln_b), out_dtypes=(F32, BF16), name="dsa_wo")
    bs = xs.shape[0]
    pos_s = jnp.full((bs,), PAST_LEN, jnp.int32)
    qs, ks, vs, qis, kis, wis = project(xs, w_in, w_tail, bs, True, *_rope_tables(pos_s, B_HEAD_DIM),
                                        *_rope_tables(pos_s, IDX_DIM), "_s")
    qs, ks, vs, qis, kis = qs[0], ks[0], vs[0], qis[0], kis[0][:, :IDX_DIM]
    qis3 = qis.reshape(bs, IDX_HEADS, IDX_DIM)
    wis3 = wis.reshape(bs, IDX_HEADS, 1)
    n_pages = page_table.shape[1]
    past_scores = _idx_scores_paged(page_table, qis3, wis3, jnp.swapaxes(cache_kidx, 1, 2),
                                    pages_per_step=DSA_PAGES_PER_STEP)
    new_page = jnp.pad(kis.reshape(bs, IDX_DIM, 1), ((0, 0), (0, 0), (0, PAGE_SIZE - 1)))
    new_scores = _idx_scores_paged(jnp.arange(bs, dtype=jnp.int32).reshape(bs, 1), qis3, wis3, new_page,
                                   pages_per_step=1)
    n_past = n_pages * PAGE_SIZE
    all_scores = jnp.concatenate(
        [past_scores.reshape(bs, n_past), new_scores[:, 0, :1], jnp.full((bs, LANES - 1), -jnp.inf, F32)], axis=1)
    bias_all = _topk_bias(all_scores, min(IDX_TOPK_MAX, (n_past + 1) // 4))
    n_pool = cache_k.shape[0]
    per_head = lambda a: jnp.repeat(a.reshape(bs, B_KV_HEADS, B_HEAD_DIM), B_GROUP, axis=1)
    os_ = _paged_attn(page_table, qs.reshape(bs, B_HEADS, B_HEAD_DIM),
                      cache_k.reshape(n_pool, PAGE_SIZE * B_KV_HEADS, B_HEAD_DIM),
                      cache_v.reshape(n_pool, PAGE_SIZE * B_KV_HEADS, B_HEAD_DIM),
                      per_head(ks), per_head(vs), mode="dsa", pages_per_step=DSA_PAGES_PER_STEP,
                      bias=jnp.repeat(bias_all[:, :n_past], B_KV_HEADS, axis=1).reshape(
                          bs, n_pages, PAGE_SIZE * B_KV_HEADS),
                      bias_new=bias_all[:, n_past:n_past + 1].reshape(bs, 1, 1))
    (xs1,) = _proj(os_.reshape(bs, dq), w_o, tm=bs, tn=D_MODEL, precise=True, ln=(xs, ln_g, ln_b),
                   name="dsa_wo_s")
    state = (k[0].reshape(b, t, B_KV_HEADS, B_HEAD_DIM), v[0].reshape(b, t, B_KV_HEADS, B_HEAD_DIM),
             ki[0][:, :IDX_DIM].reshape(b, t, IDX_DIM),
             ks.reshape(bs, 1, B_KV_HEADS, B_HEAD_DIM), vs.reshape(bs, 1, B_KV_HEADS, B_HEAD_DIM),
             kis.reshape(bs, 1, IDX_DIM))
    return x1f, x1b, xs1, state


def _pool_layer(xpf, xs, state_pool, w_group, scale, ln_g, ln_b, b, t):
    xp3 = xpf.reshape(b, t, D_MODEL)
    sub = _pool_prompt(xp3, w_group.astype(BF16), scale)
    x1f, x1b = _post_norm(xpf, sub.reshape(b * t, D_MODEL), ln_g, ln_b, tm=PROMPT_LN_TM,
                          out_dtypes=(F32, BF16))
    bs = xs.shape[0]
    xs_ext = jnp.concatenate([state_pool.astype(xs.dtype), xs.reshape(bs, 1, D_MODEL)], axis=1)
    sub_s = _pool_sample(xs_ext, w_group, scale)
    (xs1,) = _post_norm(xs, sub_s, ln_g, ln_b, tm=bs, out_dtypes=(F32,))
    return x1f, x1b, xs1, (xp3[:, -POOL_STATE_LEN:], xs_ext[:, -POOL_STATE_LEN:])


def kernel(x_prompt, x_sample, cache_l0_k, cache_l0_v, cache_l1_k, cache_l1_v, cache_l1_kidx, state_l2_pool, cache_l3_k, cache_l3_v, page_table, router_w, router_bias, l0_w_qkv, l0_w_o, l0_lam_q1, l0_lam_k1, l0_lam_q2, l0_lam_k2, l0_subln_g, l0_ln1_g, l0_ln1_b, l0_moe_w_in, l0_moe_w_down, l0_ln2_g, l0_ln2_b, l1_w_in, l1_w_o, l1_ln1_g, l1_ln1_b, l1_moe_w_in, l1_moe_w_down, l1_ln2_g, l1_ln2_b, l2_w_group, l2_scale, l2_ln1_g, l2_ln1_b, l2_moe_w_in, l2_moe_w_down, l2_ln2_g, l2_ln2_b, l3_w_qkv, l3_w_o, l3_lam_q1, l3_lam_k1, l3_lam_q2, l3_lam_k2, l3_subln_g, l3_ln1_g, l3_ln1_b, l3_moe_w_in, l3_moe_w_down, l3_ln2_g, l3_ln2_b):
    b, t, d = x_prompt.shape
    bs = x_sample.shape[0]
    xpf = x_prompt.reshape(b * t, d)
    xpb = xpf.astype(BF16)
    xs = x_sample.reshape(bs, d)
    moe_p = [(l0_moe_w_in, l0_moe_w_down, l0_ln2_g, l0_ln2_b), (l1_moe_w_in, l1_moe_w_down, l1_ln2_g, l1_ln2_b),
             (l2_moe_w_in, l2_moe_w_down, l2_ln2_g, l2_ln2_b), (l3_moe_w_in, l3_moe_w_down, l3_ln2_g, l3_ln2_b)]
    states = []
    for i in range(DEPTH):
        if i == 0:
            xpf, xpb, xs, st = _diff_layer(xpf, xpb, xs, cache_l0_k, cache_l0_v, page_table, l0_w_qkv, l0_w_o,
                                           l0_lam_q1, l0_lam_k1, l0_lam_q2, l0_lam_k2, l0_subln_g,
                                           l0_ln1_g, l0_ln1_b, i, b, t)
        elif i == 1:
            xpf, xpb, xs, st = _dsa_layer(xpf, xpb, xs, cache_l1_k, cache_l1_v, cache_l1_kidx, page_table,
                                          l1_w_in, l1_w_o, l1_ln1_g, l1_ln1_b, b, t)
        elif i == 2:
            xpf, xpb, xs, st = _pool_layer(xpf, xs, state_l2_pool, l2_w_group, l2_scale, l2_ln1_g, l2_ln1_b, b, t)
        else:
            xpf, xpb, xs, st = _diff_layer(xpf, xpb, xs, cache_l3_k, cache_l3_v, page_table, l3_w_qkv, l3_w_o,
                                           l3_lam_q1, l3_lam_k1, l3_lam_q2, l3_lam_k2, l3_subln_g,
                                           l3_ln1_g, l3_ln1_b, i, b, t)
        states.append(st)
        w_in, w_down, ln_g, ln_b = moe_p[i]
        xpf, xpb = _moe_prompt(xpf, xpb, router_w, router_bias, w_in, w_down, ln_g, ln_b)
        xs = _moe_sample(xs, router_w, router_bias, w_in, w_down, ln_g, ln_b)
    l0, l1, l2, l3 = states
    return (xpf.reshape(b, t, d), xs.reshape(bs, 1, d),
            l0[0], l0[1], l0[2], l0[3],
            l1[0], l1[1], l1[2], l1[3], l1[4], l1[5],
            l2[0], l2[1],
            l3[0], l3[1], l3[2], l3[3])
```

```python
import functools
import math

import jax
import jax.numpy as jnp
from jax import lax
from jax.experimental import pallas as pl
from jax.experimental.pallas import tpu as pltpu

F32 = jnp.float32
BF16 = jnp.bfloat16

D_MODEL = 2048
DEPTH = 4
PAST_LEN = 16384
PAGE_SIZE = 128
ROPE_THETA = 10000.0
LN_EPS = 1e-5
A_HEADS = 8
A_HEAD_DIM = D_MODEL // A_HEADS // 2
B_HEADS = 16
B_HEAD_DIM = D_MODEL // B_HEADS
B_KV_HEADS = 4
B_GROUP = B_HEADS // B_KV_HEADS
IDX_HEADS = 16
IDX_DIM = 64
IDX_TOPK_MAX = 256
POOL_WINDOWS = (2, 4, 8, 16)
POOL_GROUPS = 4
POOL_GROUP_DIM = D_MODEL // POOL_GROUPS
POOL_STATE_LEN = max(POOL_WINDOWS) - 1
N_EXPERTS = 16
N_EXPERT_GROUPS = 4
EXPERTS_PER_GROUP = N_EXPERTS // N_EXPERT_GROUPS
D_EXPERT = D_MODEL // 4
DEEPNORM_ALPHA = (2 * DEPTH) ** 0.25

LANES = 128
SUBLANES = 8
VMEM_LIMIT_BYTES = 56 * 1024 * 1024

NEG_BIG = -1e30
HIGHEST = lax.Precision.HIGHEST


def _params(*sem):
    return pltpu.CompilerParams(dimension_semantics=sem, vmem_limit_bytes=VMEM_LIMIT_BYTES)


def _dot(a, b, precise):
    if precise:
        return jnp.dot(a.astype(F32), b.astype(F32), precision=HIGHEST, preferred_element_type=F32)
    return jnp.dot(a.astype(BF16), b.astype(BF16), preferred_element_type=F32)


def _dot_nt(a, b, precise):
    dims = (((1,), (1,)), ((), ()))
    if precise:
        return lax.dot_general(a.astype(F32), b.astype(F32), dims, precision=HIGHEST,
                               preferred_element_type=F32)
    return lax.dot_general(a.astype(BF16), b.astype(BF16), dims, preferred_element_type=F32)


def _layer_norm_rows(r, g, b):
    mu = jnp.mean(r, axis=-1, keepdims=True)
    c = r - mu
    var = jnp.mean(c * c, axis=-1, keepdims=True)
    return c * lax.rsqrt(var + LN_EPS) * g + b


def _rope_tables(pos, head_dim):
    half = head_dim // 2
    inv_freq = ROPE_THETA ** (-jnp.arange(half, dtype=F32) * 2.0 / head_dim)
    ang = pos.astype(F32)[:, None] * inv_freq[None, :]
    cos, sin = jnp.cos(ang), jnp.sin(ang)
    reps = LANES // head_dim
    cos_t = jnp.tile(jnp.concatenate([cos, cos], axis=-1), (1, reps))
    sin_t = jnp.tile(jnp.concatenate([-sin, sin], axis=-1), (1, reps))
    return cos_t, sin_t


def _rope_lanes(x, cos_t, sin_t, head_dim):
    half = head_dim // 2
    if head_dim == LANES:
        partner = pltpu.roll(x, half, axis=1)
    else:
        lane = lax.broadcasted_iota(jnp.int32, x.shape, 1)
        first = (lane % head_dim) < half
        partner = jnp.where(first, pltpu.roll(x, LANES - half, axis=1), pltpu.roll(x, half, axis=1))
    return x * cos_t + partner * sin_t


def _proj_kernel(*refs, precise, rope_dim, has_ln, n_out, out_scale):
    it = iter(refs)
    x_ref, w_ref = next(it), next(it)
    cos_ref = sin_ref = res_ref = g_ref = b_ref = None
    if rope_dim:
        cos_ref, sin_ref = next(it), next(it)
    if has_ln:
        res_ref, g_ref, b_ref = next(it), next(it), next(it)
    outs = [next(it) for _ in range(n_out)]
    acc = _dot(x_ref[...], w_ref[...], precise)
    if rope_dim:
        cos_t, sin_t = cos_ref[...], sin_ref[...]
        pieces = [_rope_lanes(acc[:, c * LANES:(c + 1) * LANES], cos_t, sin_t, rope_dim)
                  for c in range(acc.shape[1] // LANES)]
        acc = jnp.concatenate(pieces, axis=1) if len(pieces) > 1 else pieces[0]
    if has_ln:
        acc = _layer_norm_rows(DEEPNORM_ALPHA * res_ref[...] + acc, g_ref[...], b_ref[...])
    if out_scale != 1.0:
        acc = acc * out_scale
    for o in outs:
        o[...] = acc.astype(o.dtype)


def _proj(x, w, *, col_start=0, n_cols=None, tm, tn, precise=False, rope=None, ln=None,
          out_dtypes=(F32,), out_scale=1.0, name="proj"):
    m, k = x.shape
    tm = min(tm, m)
    n_cols = w.shape[1] - col_start if n_cols is None else n_cols
    assert m % tm == 0 and n_cols % tn == 0 and col_start % tn == 0
    cb = col_start // tn
    grid = (n_cols // tn, m // tm)
    in_specs = [pl.BlockSpec((tm, k), lambda j, i: (i, 0)),
                pl.BlockSpec((k, tn), lambda j, i: (0, cb + j))]
    args = [x, w]
    rope_dim = 0
    if rope is not None:
        cos_t, sin_t, rope_dim = rope
        period = cos_t.shape[0] // tm
        in_specs += [pl.BlockSpec((tm, LANES), lambda j, i: (i % period, 0))] * 2
        args += [cos_t, sin_t]
    if ln is not None:
        assert tn == n_cols
        res, g, b = ln
        in_specs += [pl.BlockSpec((tm, tn), lambda j, i: (i, 0)),
                     pl.BlockSpec((1, tn), lambda j, i: (0, 0)),
                     pl.BlockSpec((1, tn), lambda j, i: (0, 0))]
        args += [res, g.reshape(1, -1), b.reshape(1, -1)]
    out_shape = [jax.ShapeDtypeStruct((m, n_cols), dt) for dt in out_dtypes]
    out_specs = [pl.BlockSpec((tm, tn), lambda j, i: (i, j)) for _ in out_dtypes]
    kern = functools.partial(_proj_kernel, precise=precise, rope_dim=rope_dim, has_ln=ln is not None,
                             n_out=len(out_dtypes), out_scale=out_scale)
    res = pl.pallas_call(kern, out_shape=out_shape, grid=grid, in_specs=in_specs, out_specs=out_specs,
                         compiler_params=_params("arbitrary", "arbitrary"), name=name)(*args)
    return res


def _lam_scalar(lam_ref, lam_init):
    v = lam_ref[...]
    a = jnp.sum(v[0:1] * v[1:2], axis=1, keepdims=True)
    b = jnp.sum(v[2:3] * v[3:4], axis=1, keepdims=True)
    return jnp.exp(a) - jnp.exp(b) + lam_init


def _subln(o, g, lam_init):
    ms = jnp.mean(o * o, axis=-1, keepdims=True)
    return o * lax.rsqrt(ms + LN_EPS) * g * (1.0 - lam_init)


def _diff_attn_kernel(lam_ref, g_ref, q_ref, k_ref, v_ref, o_ref, *, tq, lam_init, n_blocks):
    i = pl.program_id(2)
    hd = A_HEAD_DIM
    lam = _lam_scalar(lam_ref, lam_init)
    row = lax.broadcasted_iota(jnp.int32, (tq, tq), 0)
    col = lax.broadcasted_iota(jnp.int32, (tq, tq), 1)
    on_or_below_diag = col <= row

    def branch(c):
        n_left = c * tq
        q = q_ref[0]
        streams = []
        for s_ in range(2):
            lanes = slice(s_ * hd, (s_ + 1) * hd)
            qs = q[:, lanes]
            sd = _dot_nt(qs, k_ref[0, n_left:n_left + tq, lanes], False)
            sd = jnp.where(on_or_below_diag, sd, NEG_BIG)
            m = jnp.max(sd, axis=1, keepdims=True)
            if c > 0:
                sl = _dot_nt(qs, k_ref[0, 0:n_left, lanes], False)
                m = jnp.maximum(m, jnp.max(sl, axis=1, keepdims=True))
            pd = jnp.exp2(sd - m)
            l = jnp.sum(pd, axis=1, keepdims=True)
            o = _dot(pd, v_ref[0, n_left:n_left + tq, :], False)
            if c > 0:
                p_left = jnp.exp2(sl - m)
                l = l + jnp.sum(p_left, axis=1, keepdims=True)
                o = o + _dot(p_left, v_ref[0, 0:n_left, :], False)
            streams.append(o / l)
        o = streams[0] - lam * streams[1]
        o_ref[0] = _subln(o, g_ref[...], lam_init).astype(o_ref.dtype)

    for c in range(n_blocks):
        pl.when(i == c)(functools.partial(branch, c))


LOG2_E = 1.4426950408889634


def _diff_attn_prompt(q, k, v, lam_vecs, subln_g, lam_init, *, tq):
    b, t, _ = q.shape
    w = 2 * A_HEAD_DIM
    grid = (b, A_HEADS, t // tq)
    kern = functools.partial(_diff_attn_kernel, tq=tq, lam_init=lam_init, n_blocks=t // tq)
    return pl.pallas_call(
        kern, out_shape=jax.ShapeDtypeStruct((b, t, A_HEADS * w), BF16), grid=grid,
        in_specs=[pl.BlockSpec((4, A_HEAD_DIM), lambda bi, h, i: (0, 0)),
                  pl.BlockSpec((1, w), lambda bi, h, i: (0, 0)),
                  pl.BlockSpec((1, tq, w), lambda bi, h, i: (bi, i, h)),
                  pl.BlockSpec((1, t, w), lambda bi, h, i: (bi, 0, h)),
                  pl.BlockSpec((1, t, w), lambda bi, h, i: (bi, 0, h))],
        out_specs=pl.BlockSpec((1, tq, w), lambda bi, h, i: (bi, i, h)),
        compiler_params=_params("arbitrary", "arbitrary", "arbitrary"), name="diff_attn_prompt",
    )(lam_vecs, subln_g.reshape(1, w), q, k, v)


def _paged_attn_kernel(pt_ref, *refs, pages_per_step, scale, mode, lam_init):
    g_pages = pages_per_step
    it = iter(refs)
    q_ref = next(it)
    bias_ref = bias_new_ref = lam_ref = g_ref = None
    if mode == "dsa":
        bias_ref, bias_new_ref = next(it), next(it)
    else:
        lam_ref, g_ref = next(it), next(it)
    k_refs = [next(it) for _ in range(g_pages)]
    v_refs = [next(it) for _ in range(g_pages)]
    knew_ref, vnew_ref = next(it), next(it)
    o_ref = next(it)
    m_ref, l_ref, acc_ref = next(it), next(it), next(it)
    j = pl.program_id(1)
    n_steps = pl.num_programs(1)

    @pl.when(j == 0)
    def _():
        m_ref[...] = jnp.full(m_ref.shape, NEG_BIG, F32)
        l_ref[...] = jnp.zeros(l_ref.shape, F32)
        acc_ref[...] = jnp.zeros(acc_ref.shape, F32)

    q = q_ref[0]
    n_rows = q.shape[0]
    if mode == "diff":
        lanes = PAGE_SIZE * A_HEADS
        wanted = lambda r, lane: (lane % A_HEADS) == (r % A_HEADS)
    else:
        lanes = PAGE_SIZE * B_KV_HEADS
        wanted = lambda r, lane: (lane % B_KV_HEADS) == (r // B_GROUP)
    ri = lax.broadcasted_iota(jnp.int32, (n_rows, lanes), 0)
    li = lax.broadcasted_iota(jnp.int32, (n_rows, lanes), 1)
    own = wanted(ri, li)

    def update(scores, values):
        m_old = m_ref[...]
        m_new = m_old
        for s in scores:
            m_new = jnp.maximum(m_new, jnp.max(s, axis=1, keepdims=True))
        alpha = jnp.exp(m_old - m_new)
        l = alpha * l_ref[...]
        acc = alpha * acc_ref[...]
        for s, v_rows in zip(scores, values):
            p = jnp.exp(s - m_new)
            l = l + jnp.sum(p, axis=1, keepdims=True)
            acc = acc + (p * v_rows if s.shape[1] == 1 else _dot(p, v_rows, False))
        l_ref[...] = l
        acc_ref[...] = acc
        m_ref[...] = m_new

    scores = []
    for g in range(g_pages):
        if mode == "diff":
            half = n_rows // 2
            s = jnp.concatenate(
                [_dot_nt(q[c * half:(c + 1) * half], k_refs[g][0, pl.ds(c, lanes, stride=2), :], False)
                 for c in range(2)], axis=0)
        else:
            s = _dot_nt(q, k_refs[g][0], False)
        s = jnp.where(own, s * scale, NEG_BIG)
        if mode == "dsa":
            s = s + bias_ref[0, pl.ds(j * g_pages + g, 1), :]
        scores.append(s)
    update(scores, [v_refs[g][0] for g in range(g_pages)])

    @pl.when(j == n_steps - 1)
    def _():
        s_new = jnp.sum(q * knew_ref[0], axis=1, keepdims=True) * scale
        if mode == "dsa":
            s_new = s_new + bias_new_ref[0]
        update([s_new], [vnew_ref[0]])
        o = acc_ref[...] / l_ref[...]
        if mode == "diff":
            half = n_rows // 2
            o = _subln(o[:half] - _lam_scalar(lam_ref, lam_init) * o[half:], g_ref[...], lam_init)
        o_ref[0] = o


def _paged_attn(page_table, q, cache_k, cache_v, k_new, v_new, *, mode, pages_per_step,
                bias=None, bias_new=None, lam_vecs=None, subln_g=None, lam_init=0.0):
    bs, n_rows, d = q.shape
    n_pages = page_table.shape[1]
    g = pages_per_step
    assert n_pages % g == 0
    grid = (bs, n_pages // g)
    e = cache_v.shape[2]
    in_specs = [pl.BlockSpec((1, n_rows, d), lambda b, j, pt: (b, 0, 0))]
    args = [q]
    if mode == "dsa":
        in_specs += [pl.BlockSpec((1, n_pages, bias.shape[2]), lambda b, j, pt: (b, 0, 0)),
                     pl.BlockSpec((1, 1, 1), lambda b, j, pt: (b, 0, 0))]
        args += [bias, bias_new]
        out_rows = n_rows
    else:
        in_specs += [pl.BlockSpec((4, A_HEAD_DIM), lambda b, j, pt: (0, 0)),
                     pl.BlockSpec((1, e), lambda b, j, pt: (0, 0))]
        args += [lam_vecs, subln_g.reshape(1, -1)]
        out_rows = n_rows // 2

    def page_spec(arr, gi):
        return pl.BlockSpec((1,) + arr.shape[1:], lambda b, j, pt: (pt[b, j * g + gi], 0, 0))

    in_specs += [page_spec(cache_k, gi) for gi in range(g)] + [page_spec(cache_v, gi) for gi in range(g)]
    args += [cache_k] * g + [cache_v] * g
    in_specs += [pl.BlockSpec((1, n_rows, d), lambda b, j, pt: (b, 0, 0)),
                 pl.BlockSpec((1, n_rows, e), lambda b, j, pt: (b, 0, 0))]
    args += [k_new, v_new]
    kern = functools.partial(_paged_attn_kernel, pages_per_step=g, scale=d ** -0.5, mode=mode,
                             lam_init=lam_init)
    grid_spec = pltpu.PrefetchScalarGridSpec(
        num_scalar_prefetch=1, grid=grid, in_specs=in_specs,
        out_specs=pl.BlockSpec((1, out_rows, e), lambda b, j, pt: (b, 0, 0)),
        scratch_shapes=[pltpu.VMEM((n_rows, 1), F32), pltpu.VMEM((n_rows, 1), F32),
                        pltpu.VMEM((n_rows, e), F32)])
    return pl.pallas_call(kern, out_shape=jax.ShapeDtypeStruct((bs, out_rows, e), F32),
                          grid_spec=grid_spec, compiler_params=_params("arbitrary", "arbitrary"),
                          name="paged_attn_" + mode)(page_table, *args)


INT32_MIN = -2 ** 31
KEY_OF_NEG_INF = (-8388608) ^ 0x7FFFFFFF


def _topk_bias_into(bias_ref, scores, k_sel):
    r, n = scores.shape
    scores = jnp.where(scores == 0.0, 0.0, scores)
    bits = pltpu.bitcast(scores, jnp.int32)
    key = jnp.where(bits < 0, bits ^ 0x7FFFFFFF, bits)
    kf = float(k_sel)

    def step(t, ans):
        cand = ans + jnp.left_shift(jnp.int32(1), 31 - t)
        cnt = jnp.sum(jnp.where(key >= cand, 1.0, 0.0), axis=1, keepdims=True)
        return jnp.where(cnt >= kf, cand, ans)

    thr = lax.fori_loop(0, 32, step, jnp.full((r, 1), INT32_MIN, jnp.int32), unroll=4)
    gt = key > thr
    eq = key == thr
    cnt_gt = jnp.sum(jnp.where(gt, 1.0, 0.0), axis=1, keepdims=True)
    cnt_eq = jnp.sum(jnp.where(eq, 1.0, 0.0), axis=1, keepdims=True)
    bias_ref[...] = jnp.where(key >= thr, 0.0, NEG_BIG)
    tie = jnp.max(jnp.where((cnt_gt + cnt_eq > kf) & (thr > KEY_OF_NEG_INF), 1.0, 0.0))

    @pl.when(tie > 0.0)
    def _():
        need = kf - cnt_gt
        ri = lax.broadcasted_iota(jnp.int32, (LANES, LANES), 0)
        ci = lax.broadcasted_iota(jnp.int32, (LANES, LANES), 1)
        tri = jnp.where(ri < ci, 1.0, 0.0).astype(BF16)
        run = jnp.zeros((r, 1), F32)
        for c in range(n // LANES):
            sl = slice(c * LANES, (c + 1) * LANES)
            eq_c = jnp.where(eq[:, sl], 1.0, 0.0)
            before = jnp.dot(eq_c.astype(BF16), tri, preferred_element_type=F32) + run
            take = gt[:, sl] | (eq[:, sl] & (before < need))
            bias_ref[:, sl] = jnp.where(take, 0.0, NEG_BIG)
            run = run + jnp.sum(eq_c, axis=1, keepdims=True)


def _dsa_prompt_kernel(qi_ref, kibd_ref, wi_ref, q_ref, k_ref, v_ref, o_ref, bias_ref, *, tq, k_sel,
                       width_step):
    i = pl.program_id(1)
    t = k_ref.shape[1]

    def branch(n):
        wi = wi_ref[0]
        qi = qi_ref[0]
        score = jnp.zeros((tq, n), F32)
        for hp in range(IDX_HEADS // 2):
            q_pair = qi[:, hp * LANES:(hp + 1) * LANES]
            for c in range(2):
                d = _dot_nt(q_pair, kibd_ref[0, c * t:c * t + n, :], False)
                score = score + wi[:, 2 * hp + c:2 * hp + c + 1] * jnp.maximum(d, 0.0)
        row = i * tq + lax.broadcasted_iota(jnp.int32, (tq, n), 0)
        col = lax.broadcasted_iota(jnp.int32, (tq, n), 1)
        causal = col <= row
        bias_view = bias_ref.at[:, pl.ds(0, n)]
        _topk_bias_into(bias_view, jnp.where(causal, score, -jnp.inf), k_sel)
        bias = jnp.where(causal, bias_view[...], NEG_BIG)
        bias4 = jnp.concatenate([bias] * B_GROUP, axis=0)
        q = q_ref[0]
        outs = []
        for kv in range(B_KV_HEADS):
            qg = jnp.concatenate([q[:, (kv * B_GROUP + r) * B_HEAD_DIM:(kv * B_GROUP + r + 1) * B_HEAD_DIM]
                                  for r in range(B_GROUP)], axis=0)
            kg = k_ref[0, 0:n, kv * B_HEAD_DIM:(kv + 1) * B_HEAD_DIM]
            vg = v_ref[0, 0:n, kv * B_HEAD_DIM:(kv + 1) * B_HEAD_DIM]
            s = _dot_nt(qg, kg, False) + bias4
            p = jnp.exp2(s - jnp.max(s, axis=1, keepdims=True))
            l = jnp.sum(p, axis=1, keepdims=True)
            og = _dot(p, vg, False) / l
            outs += [og[r * tq:(r + 1) * tq] for r in range(B_GROUP)]
        o_ref[0] = jnp.concatenate(outs, axis=1).astype(o_ref.dtype)

    for cls in range(t // width_step):
        pl.when((i * tq + tq - 1) // width_step == cls)(functools.partial(branch, (cls + 1) * width_step))


def _dsa_prompt(qi, ki, wi, q, k, v, *, tq):
    b, t, _ = q.shape
    k_sel = min(IDX_TOPK_MAX, t // 4)
    zeros = jnp.zeros_like(ki)
    kibd = jnp.concatenate([jnp.concatenate([ki, zeros], axis=-1),
                            jnp.concatenate([zeros, ki], axis=-1)], axis=1)
    kvw = B_KV_HEADS * B_HEAD_DIM
    kern = functools.partial(_dsa_prompt_kernel, tq=tq, k_sel=k_sel, width_step=min(DSA_WIDTH_STEP, t))
    return pl.pallas_call(
        kern, out_shape=jax.ShapeDtypeStruct((b, t, D_MODEL), BF16), grid=(b, t // tq),
        in_specs=[pl.BlockSpec((1, tq, IDX_HEADS * IDX_DIM), lambda bi, i: (bi, i, 0)),
                  pl.BlockSpec((1, 2 * t, LANES), lambda bi, i: (bi, 0, 0)),
                  pl.BlockSpec((1, tq, IDX_HEADS), lambda bi, i: (bi, i, 0)),
                  pl.BlockSpec((1, tq, D_MODEL), lambda bi, i: (bi, i, 0)),
                  pl.BlockSpec((1, t, kvw), lambda bi, i: (bi, 0, 0)),
                  pl.BlockSpec((1, t, kvw), lambda bi, i: (bi, 0, 0))],
        out_specs=pl.BlockSpec((1, tq, D_MODEL), lambda bi, i: (bi, i, 0)),
        scratch_shapes=[pltpu.VMEM((tq, t), F32)],
        compiler_params=_params("arbitrary", "arbitrary"), name="dsa_prompt",
    )(qi, kibd, wi, q, k, v)


def _idx_scores_kernel(pt_ref, qi_ref, w_ref, *refs, pages_per_step):
    page_refs, o_ref = refs[:pages_per_step], refs[pages_per_step]
    qi = qi_ref[0]
    w = w_ref[0]
    rows = []
    for g in range(pages_per_step):
        d = _dot(qi, page_refs[g][0], True)
        rows.append(jnp.sum(w * jnp.maximum(d, 0.0), axis=0, keepdims=True))
    o_ref[0] = jnp.concatenate(rows, axis=0)


def _idx_scores_paged(page_table, qi, wi, cache_kidx, *, pages_per_step):
    bs = qi.shape[0]
    n_pages = page_table.shape[1]
    g = pages_per_step

    def page_spec(gi):
        return pl.BlockSpec((1, IDX_DIM, PAGE_SIZE), lambda b, j, pt: (pt[b, j * g + gi], 0, 0))

    grid_spec = pltpu.PrefetchScalarGridSpec(
        num_scalar_prefetch=1, grid=(bs, n_pages // g),
        in_specs=[pl.BlockSpec((1, IDX_HEADS, IDX_DIM), lambda b, j, pt: (b, 0, 0)),
                  pl.BlockSpec((1, IDX_HEADS, 1), lambda b, j, pt: (b, 0, 0))]
        + [page_spec(gi) for gi in range(g)],
        out_specs=pl.BlockSpec((1, g, PAGE_SIZE), lambda b, j, pt: (b, j, 0)))
    kern = functools.partial(_idx_scores_kernel, pages_per_step=g)
    return pl.pallas_call(kern, out_shape=jax.ShapeDtypeStruct((bs, n_pages, PAGE_SIZE), F32),
                          grid_spec=grid_spec, compiler_params=_params("arbitrary", "arbitrary"),
                          name="dsa_idx_scores")(page_table, qi, wi, *([cache_kidx] * g))


def _topk_bias_kernel(s_ref, o_ref, *, k_sel):
    _topk_bias_into(o_ref, s_ref[...], k_sel)


def _topk_bias(scores, k_sel):
    return pl.pallas_call(functools.partial(_topk_bias_kernel, k_sel=k_sel),
                          out_shape=jax.ShapeDtypeStruct(scores.shape, F32),
                          compiler_params=pltpu.CompilerParams(vmem_limit_bytes=VMEM_LIMIT_BYTES),
                          name="topk_bias")(scores)


def _pool_prompt_kernel(x_ref, w_ref, sc_ref, o_ref, pad_ref, *, window, rows):
    t = x_ref.shape[1]
    halo = 2 * SUBLANES
    pad_ref[0:halo, :] = jnp.zeros((halo, pad_ref.shape[1]), F32)
    pad_ref[halo:, :] = x_ref[0]
    w = w_ref[0]
    sc = sc_ref[...]

    def body(c, carry):
        r0 = pl.multiple_of(c * rows, rows)
        xh = pad_ref[pl.ds(r0, rows + halo), :]
        x = xh[halo:, :]
        win = x
        for jj in range(1, window):
            win = win + xh[halo - jj:halo - jj + rows, :]
        pos = r0 + lax.broadcasted_iota(jnp.int32, (rows, 1), 0)
        count = jnp.minimum(window, pos + 1).astype(F32)
        mixed = win / count - x
        o_ref[0, pl.ds(r0, rows), :] = _dot(mixed, w, False) * sc
        return carry

    lax.fori_loop(0, t // rows, body, 0)


def _pool_prompt(x, w_group_bf16, scale):
    b, t, d = x.shape
    gd = POOL_GROUP_DIM
    outs = []
    for g, window in enumerate(POOL_WINDOWS):
        kern = functools.partial(_pool_prompt_kernel, window=window, rows=256)
        outs.append(pl.pallas_call(
            kern, out_shape=jax.ShapeDtypeStruct((b, t, gd), F32), grid=(b,),
            in_specs=[pl.BlockSpec((1, t, gd), lambda bi, g=g: (bi, 0, g)),
                      pl.BlockSpec((1, gd, gd), lambda bi, g=g: (g, 0, 0)),
                      pl.BlockSpec((1, gd), lambda bi, g=g: (0, g))],
            out_specs=pl.BlockSpec((1, t, gd), lambda bi: (bi, 0, 0)),
            scratch_shapes=[pltpu.VMEM((t + 2 * SUBLANES, gd), F32)],
            compiler_params=_params("arbitrary"), name="pool_prompt_w%d" % window,
        )(x, w_group_bf16, scale.reshape(1, d)))
    return jnp.concatenate(outs, axis=-1)


def _pool_sample_kernel(x_ref, w_ref, sc_ref, o_ref):
    g = pl.program_id(0)
    window = jnp.left_shift(jnp.int32(2), g)
    x = x_ref[...]
    n = x.shape[1]
    rowi = lax.broadcasted_iota(jnp.int32, x.shape, 1)
    win = jnp.sum(jnp.where(rowi >= n - window, x, 0.0), axis=1)
    mixed = win / window.astype(F32) - x[:, n - 1, :]
    o_ref[...] = _dot(mixed, w_ref[0], True) * sc_ref[...]


def _pool_sample(x_ext, w_group, scale):
    b, n, d = x_ext.shape
    gd = POOL_GROUP_DIM
    return pl.pallas_call(
        _pool_sample_kernel, out_shape=jax.ShapeDtypeStruct((b, d), F32), grid=(POOL_GROUPS,),
        in_specs=[pl.BlockSpec((b, n, gd), lambda g: (0, 0, g)),
                  pl.BlockSpec((1, gd, gd), lambda g: (g, 0, 0)),
                  pl.BlockSpec((1, gd), lambda g: (0, g))],
        out_specs=pl.BlockSpec((b, gd), lambda g: (0, g)),
        compiler_params=_params("arbitrary"), name="pool_sample",
    )(x_ext, w_group, scale.reshape(1, d))


def _split_bf16(a):
    hi = a.astype(BF16)
    return hi, (a - hi.astype(F32)).astype(BF16)


def _router_kernel(x_ref, rw_ref, rb_ref, g_ref, *, full_f32):
    tm = x_ref.shape[0]
    if full_f32:
        logits = _dot(x_ref[...], rw_ref[...], True)
    else:
        x_hi, x_lo = _split_bf16(x_ref[...])
        w_hi, w_lo = _split_bf16(rw_ref[...])
        logits = (jnp.dot(x_hi, w_hi, preferred_element_type=F32)
                  + jnp.dot(x_lo, w_hi, preferred_element_type=F32)
                  + jnp.dot(x_hi, w_lo, preferred_element_type=F32))
    lane = lax.broadcasted_iota(jnp.int32, logits.shape, 1)
    logits = jnp.where(lane < N_EXPERTS, logits, NEG_BIG)
    e = jnp.exp(logits - jnp.max(logits, axis=1, keepdims=True))
    aff_t = (e / jnp.sum(e, axis=1, keepdims=True)).T
    bias = rb_ref[...]
    aff = [aff_t[x:x + 1, :] for x in range(N_EXPERTS)]
    sel = [aff[x] + bias[x:x + 1, :] for x in range(N_EXPERTS)]
    gsz = EXPERTS_PER_GROUP
    best_score = best_group = None
    for g in range(N_EXPERT_GROUPS):
        a, b, c, d = sel[g * gsz:(g + 1) * gsz]
        hi1, lo1, hi2, lo2 = jnp.maximum(a, b), jnp.minimum(a, b), jnp.maximum(c, d), jnp.minimum(c, d)
        score = jnp.maximum(hi1, hi2) + jnp.maximum(jnp.minimum(hi1, hi2), jnp.maximum(lo1, lo2))
        if g == 0:
            best_score, best_group = score, jnp.zeros(score.shape, jnp.int32)
        else:
            better = score > best_score
            best_group = jnp.where(better, g, best_group)
            best_score = jnp.where(better, score, best_score)
    chosen = []
    for x in range(N_EXPERTS):
        g = x // gsz
        rank = jnp.zeros((1, tm), F32)
        for y in range(g * gsz, (g + 1) * gsz):
            if y == x:
                continue
            ahead = (sel[y] > sel[x]) | ((sel[y] == sel[x]) & (y < x))
            rank = rank + jnp.where(ahead, 1.0, 0.0)
        chosen.append((best_group == g) & (rank < 2.0))
    top_sum = jnp.zeros((1, tm), F32)
    for x in range(N_EXPERTS):
        top_sum = top_sum + jnp.where(chosen[x], aff[x], 0.0)
    rows = [jnp.where(chosen[x], aff[x] / top_sum, 0.0) for x in range(N_EXPERTS)]
    rows += [jnp.where(chosen[x], 1.0, 0.0) for x in range(N_EXPERTS)]
    rows.append(jnp.zeros((LANES - 2 * N_EXPERTS, tm), F32))
    g_ref[...] = jnp.concatenate(rows, axis=0).T


def _router(x, router_w, router_bias, *, tm, full_f32):
    m, d = x.shape
    tm = min(tm, m)
    rw = jnp.pad(router_w, ((0, 0), (0, LANES - N_EXPERTS)))
    return pl.pallas_call(
        functools.partial(_router_kernel, full_f32=full_f32),
        out_shape=jax.ShapeDtypeStruct((m, LANES), F32), grid=(m // tm,),
        in_specs=[pl.BlockSpec((tm, d), lambda i: (i, 0)),
                  pl.BlockSpec((d, LANES), lambda i: (0, 0)),
                  pl.BlockSpec((N_EXPERTS, 1), lambda i: (0, 0))],
        out_specs=pl.BlockSpec((tm, LANES), lambda i: (i, 0)),
        compiler_params=_params("arbitrary"), name="router",
    )(x, rw, router_bias.reshape(N_EXPERTS, 1))


def _moe_dense_kernel(ids_ref, na_ref, x_ref, g_ref, win_ref, wdn_ref, res_ref, lg_ref, lb_ref, *refs,
                      precise):
    out_refs, acc_ref = refs[:-1], refs[-1]
    e = pl.program_id(1)

    @pl.when(e == 0)
    def _():
        acc_ref[...] = jnp.zeros(acc_ref.shape, F32)

    @pl.when(e < na_ref[0])
    def _():
        h = _dot(x_ref[...], win_ref[0], precise)
        hg, hu = h[:, :D_EXPERT], h[:, D_EXPERT:]
        gates = g_ref[...]
        lane = lax.broadcasted_iota(jnp.int32, gates.shape, 1)
        gate = jnp.sum(jnp.where(lane == ids_ref[e], gates, 0.0), axis=1, keepdims=True)
        a = hg * jax.nn.sigmoid(hg) * hu * gate
        acc_ref[...] += _dot(a, wdn_ref[0], precise)

    @pl.when(e == pl.num_programs(1) - 1)
    def _():
        y = _layer_norm_rows(DEEPNORM_ALPHA * res_ref[...] + acc_ref[...], lg_ref[...], lb_ref[...])
        for o in out_refs:
            o[...] = y.astype(o.dtype)


def _moe_dense(x, routed, w_in, w_down, res, ln_g, ln_b, *, tm, precise, out_dtypes):
    m, d = x.shape
    tm = min(tm, m)
    active = jnp.any(routed[:, N_EXPERTS:2 * N_EXPERTS] > 0.5, axis=0)
    order = jnp.argsort(jnp.logical_not(active), stable=True).astype(jnp.int32)
    n_act = jnp.sum(active.astype(jnp.int32))
    ids = jnp.where(jnp.arange(N_EXPERTS) < n_act, order, order[jnp.maximum(n_act - 1, 0)])
    kern = functools.partial(_moe_dense_kernel, precise=precise)
    grid_spec = pltpu.PrefetchScalarGridSpec(
        num_scalar_prefetch=2, grid=(m // tm, N_EXPERTS),
        in_specs=[pl.BlockSpec((tm, d), lambda i, e, ids, na: (i, 0)),
                  pl.BlockSpec((tm, LANES), lambda i, e, ids, na: (i, 0)),
                  pl.BlockSpec((1, d, 2 * D_EXPERT), lambda i, e, ids, na: (ids[e], 0, 0)),
                  pl.BlockSpec((1, D_EXPERT, d), lambda i, e, ids, na: (ids[e], 0, 0)),
                  pl.BlockSpec((tm, d), lambda i, e, ids, na: (i, 0)),
                  pl.BlockSpec((1, d), lambda i, e, ids, na: (0, 0)),
                  pl.BlockSpec((1, d), lambda i, e, ids, na: (0, 0))],
        out_specs=[pl.BlockSpec((tm, d), lambda i, e, ids, na: (i, 0)) for _ in out_dtypes],
        scratch_shapes=[pltpu.VMEM((tm, d), F32)])
    return pl.pallas_call(
        kern, out_shape=[jax.ShapeDtypeStruct((m, d), dt) for dt in out_dtypes], grid_spec=grid_spec,
        compiler_params=_params("arbitrary", "arbitrary"), name="moe_dense",
    )(ids, n_act.reshape(1), x, routed, w_in, w_down, res, ln_g.reshape(1, d), ln_b.reshape(1, d))


def _moe_plan(gates_and_mask, tile):
    gates = gates_and_mask[:, :N_EXPERTS]
    chosen = gates_and_mask[:, N_EXPERTS:2 * N_EXPERTS] > 0.5
    m = gates.shape[0]
    c = chosen.astype(jnp.int32)
    rank = jnp.cumsum(c, axis=0) - c
    counts = jnp.sum(c, axis=0)
    padded = ((counts + tile - 1) // tile) * tile
    ends = jnp.cumsum(padded)
    slot = (ends - padded)[None, :] + rank
    slot_a = jnp.min(jnp.where(chosen, slot, jnp.int32(2 ** 30)), axis=1)
    slot_b = jnp.max(jnp.where(chosen, slot, jnp.int32(-1)), axis=1)
    gate_a = jnp.sum(jnp.where(chosen & (slot == slot_a[:, None]), gates, 0.0), axis=1)
    gate_b = jnp.sum(jnp.where(chosen & (slot == slot_b[:, None]), gates, 0.0), axis=1)
    n_tiles = 2 * m // tile + N_EXPERTS
    starts = jnp.arange(n_tiles, dtype=jnp.int32) * tile
    tile_expert = jnp.minimum(jnp.sum((starts[:, None] >= ends[None, :]).astype(jnp.int32), axis=1),
                              N_EXPERTS - 1)
    n_used = (ends[-1] // tile).astype(jnp.int32).reshape(1)
    return (jnp.stack([slot_a, slot_b], axis=1), jnp.stack([gate_a, gate_b], axis=1),
            tile_expert, n_used, n_tiles, ends.astype(jnp.int32), padded.astype(jnp.int32))


def _row_copy(src, src_row, dst, dst_row, sem):
    return pltpu.make_async_copy(src.at[pl.ds(src_row, 1)], dst.at[pl.ds(dst_row, 1)], sem)


def _moe_dispatch_kernel(ends_ref, padded_ref, slots_ref, x_ref, xs_ref, zero_ref, stage_ref, zero_sem, sems,
                         *, expert_tile):
    i = pl.program_id(0)
    t = slots_ref.shape[2] // 2

    @pl.when(i == 0)
    def _():
        zero_ref[...] = jnp.zeros(zero_ref.shape, zero_ref.dtype)

        def clear_tile(first_row):
            first = pl.multiple_of(first_row, expert_tile)
            return pltpu.make_async_copy(zero_ref, xs_ref.at[pl.ds(first, expert_tile)], zero_sem)

        def clear_all(act):
            for e in range(N_EXPERTS):
                pl.when(padded_ref[e] > 0)(lambda e=e: act(clear_tile(ends_ref[e] - expert_tile)))
            for j in range(xs_ref.shape[0] // expert_tile):
                pl.when(j * expert_tile >= ends_ref[N_EXPERTS - 1])(
                    lambda j=j: act(clear_tile(j * expert_tile)))

        clear_all(lambda cp: cp.start())
        clear_all(lambda cp: cp.wait())

    cur = lax.rem(i, 2)
    stage_ref[cur] = x_ref[...]

    def start(r, carry):
        for k in range(2):
            _row_copy(stage_ref.at[cur], r, xs_ref, slots_ref[0, 0, k * t + r], sems.at[cur]).start()
        return carry

    lax.fori_loop(0, t, start, 0, unroll=4)

    def drain(slot):
        rows = xs_ref.at[pl.ds(0, 2 * t)]
        pltpu.make_async_copy(rows, rows, sems.at[slot]).wait()

    pl.when(i > 0)(lambda: drain(1 - cur))
    pl.when(i == pl.num_programs(0) - 1)(lambda: drain(cur))


def _moe_dispatch(x, slots, ends, padded, n_rows, *, tile, expert_tile):
    m, d = x.shape
    nt = m // tile
    slots_t = slots.reshape(nt, tile, 2).transpose(0, 2, 1).reshape(nt, 1, 2 * tile)
    grid_spec = pltpu.PrefetchScalarGridSpec(
        num_scalar_prefetch=2, grid=(nt,),
        in_specs=[pl.BlockSpec((1, 1, 2 * tile), lambda i, en, pa: (i, 0, 0), memory_space=pltpu.SMEM),
                  pl.BlockSpec((tile, d), lambda i, en, pa: (i, 0))],
        out_specs=pl.BlockSpec(memory_space=pl.ANY),
        scratch_shapes=[pltpu.VMEM((expert_tile, d), x.dtype), pltpu.VMEM((2, tile, d), x.dtype),
                        pltpu.SemaphoreType.DMA, pltpu.SemaphoreType.DMA((2,))])
    kern = functools.partial(_moe_dispatch_kernel, expert_tile=expert_tile)
    return pl.pallas_call(kern, out_shape=jax.ShapeDtypeStruct((n_rows, d), x.dtype), grid_spec=grid_spec,
                          compiler_params=_params("arbitrary"), name="moe_dispatch")(ends, padded, slots_t, x)


def _moe_ffn_kernel(te_ref, nu_ref, xs_ref, win_ref, wdn_ref, y_ref, win_bf, wdn_bf):
    i = pl.program_id(0)

    @pl.when(i < nu_ref[0])
    def _():
        prev = te_ref[jnp.maximum(i - 1, 0)]

        @pl.when((i == 0) | (te_ref[i] != prev))
        def _():
            win_bf[...] = win_ref[0].astype(BF16)
            wdn_bf[...] = wdn_ref[0].astype(BF16)

        h = _dot(xs_ref[...], win_bf[...], False)
        hg, hu = h[:, :D_EXPERT], h[:, D_EXPERT:]
        y_ref[...] = _dot(hg * jax.nn.sigmoid(hg) * hu, wdn_bf[...], False)

    @pl.when(i >= nu_ref[0])
    def _():
        y_ref[...] = jnp.zeros(y_ref.shape, F32)


def _moe_ffn(xs, tile_expert, n_used, w_in, w_down, *, tile):
    n_rows, d = xs.shape
    n_tiles = n_rows // tile
    grid_spec = pltpu.PrefetchScalarGridSpec(
        num_scalar_prefetch=2, grid=(n_tiles,),
        in_specs=[pl.BlockSpec((tile, d), lambda i, te, nu: (jnp.minimum(i, nu[0] - 1), 0)),
                  pl.BlockSpec((1, d, 2 * D_EXPERT), lambda i, te, nu: (te[i], 0, 0)),
                  pl.BlockSpec((1, D_EXPERT, d), lambda i, te, nu: (te[i], 0, 0))],
        out_specs=pl.BlockSpec((tile, d), lambda i, te, nu: (i, 0)),
        scratch_shapes=[pltpu.VMEM((d, 2 * D_EXPERT), BF16), pltpu.VMEM((D_EXPERT, d), BF16)])
    return pl.pallas_call(_moe_ffn_kernel, out_shape=jax.ShapeDtypeStruct((n_rows, d), F32),
                          grid_spec=grid_spec, compiler_params=_params("arbitrary"),
                          name="moe_ffn")(tile_expert, n_used, xs, w_in, w_down)


def _moe_combine_kernel(slots_ref, next_slots_ref, y_ref, g_ref, res_ref, lg_ref, lb_ref, *refs):
    out_refs, (ya_ref, yb_ref, sems) = refs[:-3], refs[-3:]
    i = pl.program_id(0)
    t = res_ref.shape[0]
    cur = lax.rem(i, 2)

    def gather(s_ref, buf):
        def start(r, carry):
            _row_copy(y_ref, s_ref[0, 0, r], ya_ref.at[buf], r, sems.at[buf]).start()
            _row_copy(y_ref, s_ref[0, 0, t + r], yb_ref.at[buf], r, sems.at[buf]).start()
            return carry

        lax.fori_loop(0, t, start, 0, unroll=4)

    pl.when(i == 0)(lambda: gather(slots_ref, 0))
    pl.when(i + 1 < pl.num_programs(0))(lambda: gather(next_slots_ref, 1 - cur))
    for buf_ref in (ya_ref, yb_ref):
        pltpu.make_async_copy(y_ref.at[pl.ds(0, t)], buf_ref.at[cur], sems.at[cur]).wait()
    g = g_ref[...]
    sub = g[:, 0:1] * ya_ref[cur] + g[:, 1:2] * yb_ref[cur]
    y = _layer_norm_rows(DEEPNORM_ALPHA * res_ref[...] + sub, lg_ref[...], lb_ref[...])
    for o in out_refs:
        o[...] = y.astype(o.dtype)


def _moe_combine(y, slots, gates2, res, ln_g, ln_b, *, tile, out_dtypes):
    m, d = res.shape
    nt = m // tile
    slots_t = slots.reshape(nt, tile, 2).transpose(0, 2, 1).reshape(nt, 1, 2 * tile)
    row = pl.BlockSpec((tile, d), lambda i: (i, 0))
    vec = pl.BlockSpec((1, d), lambda i: (0, 0))
    return pl.pallas_call(
        _moe_combine_kernel, out_shape=[jax.ShapeDtypeStruct((m, d), dt) for dt in out_dtypes], grid=(nt,),
        in_specs=[pl.BlockSpec((1, 1, 2 * tile), lambda i: (i, 0, 0), memory_space=pltpu.SMEM),
                  pl.BlockSpec((1, 1, 2 * tile), lambda i: (jnp.minimum(i + 1, nt - 1), 0, 0),
                               memory_space=pltpu.SMEM),
                  pl.BlockSpec(memory_space=pl.ANY),
                  pl.BlockSpec((tile, 2), lambda i: (i, 0)), row, vec, vec],
        out_specs=[row for _ in out_dtypes],
        scratch_shapes=[pltpu.VMEM((2, tile, d), F32), pltpu.VMEM((2, tile, d), F32),
                        pltpu.SemaphoreType.DMA((2,))],
        compiler_params=_params("arbitrary"), name="moe_combine",
    )(slots_t, slots_t, y, gates2, res, ln_g.reshape(1, d), ln_b.reshape(1, d))


def _post_norm_kernel(x_ref, sub_ref, g_ref, b_ref, *out_refs):
    y = _layer_norm_rows(DEEPNORM_ALPHA * x_ref[...] + sub_ref[...], g_ref[...], b_ref[...])
    for o in out_refs:
        o[...] = y.astype(o.dtype)


def _post_norm(x, sub, g, b, *, tm, out_dtypes):
    m, d = x.shape
    tm = min(tm, m)
    row = pl.BlockSpec((tm, d), lambda i: (i, 0))
    vec = pl.BlockSpec((1, d), lambda i: (0, 0))
    return pl.pallas_call(
        _post_norm_kernel, out_shape=[jax.ShapeDtypeStruct((m, d), dt) for dt in out_dtypes],
        grid=(m // tm,), in_specs=[row, row, vec, vec], out_specs=[row for _ in out_dtypes],
        compiler_params=_params("arbitrary"), name="post_norm",
    )(x, sub, g.reshape(1, d), b.reshape(1, d))


PROMPT_TM = 1024
PROMPT_LN_TM = 256
MOE_TM = 512
MOE_ROW_TILE = 256
ATTN_TQ = 256
DSA_TQ = 128
DSA_WIDTH_STEP = 512
DIFF_PAGES_PER_STEP = 8
DSA_PAGES_PER_STEP = 16


def _moe_prompt(xf, xb, router_w, router_bias, w_in, w_down, ln_g, ln_b):
    del xb
    routed = _router(xf, router_w, router_bias, tm=PROMPT_TM, full_f32=False)
    slots, gates2, tile_expert, n_used, n_tiles, ends, padded = _moe_plan(routed, MOE_TM)
    xs = _moe_dispatch(xf, slots, ends, padded, n_tiles * MOE_TM, tile=MOE_ROW_TILE, expert_tile=MOE_TM)
    y = _moe_ffn(xs, tile_expert, n_used, w_in, w_down, tile=MOE_TM)
    return _moe_combine(y, slots, gates2, xf, ln_g, ln_b, tile=MOE_ROW_TILE, out_dtypes=(F32, BF16))


def _moe_sample(xs, router_w, router_bias, w_in, w_down, ln_g, ln_b):
    n = xs.shape[0]
    xpad = jnp.pad(xs, ((0, LANES - n), (0, 0)))
    gates = _router(xpad, router_w, router_bias, tm=LANES, full_f32=True)[:n]
    (y,) = _moe_dense(xs, gates, w_in, w_down, xs, ln_g, ln_b, tm=n, precise=True, out_dtypes=(F32,))
    return y


def _diff_layer(xpf, xpb, xs, cache_k, cache_v, page_table, w_qkv, w_o, lq1, lk1, lq2, lk2, subln_g,
                ln_g, ln_b, layer_idx, b, t):
    lam_init = 0.8 - 0.6 * math.exp(-0.3 * layer_idx)
    lam_vecs = jnp.stack([lq1, lk1, lq2, lk2]).astype(F32)
    dqk = 2 * A_HEADS * A_HEAD_DIM
    cos_p, sin_p = _rope_tables(jnp.arange(t, dtype=jnp.int32), A_HEAD_DIM)
    wb = w_qkv.astype(BF16)
    tm = PROMPT_TM
    (qb,) = _proj(xpb, wb, col_start=0, n_cols=dqk, tm=tm, tn=1024, rope=(cos_p, sin_p, A_HEAD_DIM),
                  out_dtypes=(BF16,), out_scale=A_HEAD_DIM ** -0.5 * LOG2_E, name="diff_q")
    kf, kb = _proj(xpb, wb, col_start=dqk, n_cols=dqk, tm=tm, tn=1024, rope=(cos_p, sin_p, A_HEAD_DIM),
                   out_dtypes=(F32, BF16), name="diff_k")
    vf, vb = _proj(xpb, wb, col_start=2 * dqk, n_cols=dqk, tm=tm, tn=1024, out_dtypes=(F32, BF16),
                   name="diff_v")
    o = _diff_attn_prompt(qb.reshape(b, t, dqk), kb.reshape(b, t, dqk), vb.reshape(b, t, dqk),
                          lam_vecs, subln_g, lam_init, tq=ATTN_TQ)
    x1f, x1b = _proj(o.reshape(b * t, dqk), w_o.astype(BF16), tm=PROMPT_LN_TM, tn=D_MODEL,
                     ln=(xpf, ln_g, ln_b), out_dtypes=(F32, BF16), name="diff_wo")
    bs = xs.shape[0]
    pos_s = jnp.full((bs,), PAST_LEN, jnp.int32)
    cos_s, sin_s = _rope_tables(pos_s, A_HEAD_DIM)
    (qs,) = _proj(xs, w_qkv, col_start=0, n_cols=dqk, tm=bs, tn=1024, precise=True,
                  rope=(cos_s, sin_s, A_HEAD_DIM), name="diff_q_s")
    (ks,) = _proj(xs, w_qkv, col_start=dqk, n_cols=dqk, tm=bs, tn=1024, precise=True,
                  rope=(cos_s, sin_s, A_HEAD_DIM), name="diff_k_s")
    (vs,) = _proj(xs, w_qkv, col_start=2 * dqk, n_cols=dqk, tm=bs, tn=1024, precise=True, name="diff_v_s")
    n_pool = cache_k.shape[0]

    def streams_first(a):
        a = a.reshape(bs, A_HEADS, 2, A_HEAD_DIM)
        return a.transpose(0, 2, 1, 3).reshape(bs, 2 * A_HEADS, A_HEAD_DIM)

    vs_rows = vs.reshape(bs, A_HEADS, 2 * A_HEAD_DIM)
    os_ = _paged_attn(page_table, streams_first(qs),
                      cache_k.reshape(n_pool, PAGE_SIZE * 2 * A_HEADS, A_HEAD_DIM),
                      cache_v.reshape(n_pool, PAGE_SIZE * A_HEADS, 2 * A_HEAD_DIM),
                      streams_first(ks), jnp.concatenate([vs_rows, vs_rows], axis=1),
                      mode="diff", pages_per_step=DIFF_PAGES_PER_STEP, lam_vecs=lam_vecs, subln_g=subln_g,
                      lam_init=lam_init)
    (xs1,) = _proj(os_.reshape(bs, dqk), w_o, tm=bs, tn=D_MODEL, precise=True, ln=(xs, ln_g, ln_b),
                   name="diff_wo_s")
    state = (kf.reshape(b, t, 2 * A_HEADS, A_HEAD_DIM), vf.reshape(b, t, A_HEADS, 2 * A_HEAD_DIM),
             ks.reshape(bs, 1, 2 * A_HEADS, A_HEAD_DIM), vs.reshape(bs, 1, A_HEADS, 2 * A_HEAD_DIM))
    return x1f, x1b, xs1, state


def _dsa_layer(xpf, xpb, xs, cache_k, cache_v, cache_kidx, page_table, w_in, w_o, ln_g, ln_b, b, t):
    dq = B_HEADS * B_HEAD_DIM
    dkv = B_KV_HEADS * B_HEAD_DIM
    dqi = IDX_HEADS * IDX_DIM
    c_k, c_v, c_qi, c_tail = dq, dq + dkv, dq + 2 * dkv, dq + 2 * dkv + dqi
    w_tail = jnp.pad(w_in[:, c_tail:], ((0, 0), (0, LANES - (IDX_DIM + IDX_HEADS))))

    def project(x, w, wt, tm, precise, cos_h, sin_h, cos_i, sin_i, tag):
        both = (F32,) if precise else (F32, BF16)
        low = (F32,) if precise else (BF16,)
        kw = dict(tm=tm, precise=precise)
        q = _proj(x, w, col_start=0, n_cols=dq, tn=1024, rope=(cos_h, sin_h, B_HEAD_DIM), out_dtypes=low,
                  out_scale=1.0 if precise else B_HEAD_DIM ** -0.5 * LOG2_E, name="dsa_q" + tag, **kw)
        k = _proj(x, w, col_start=c_k, n_cols=dkv, tn=dkv, rope=(cos_h, sin_h, B_HEAD_DIM),
                  out_dtypes=both, name="dsa_k" + tag, **kw)
        v = _proj(x, w, col_start=c_v, n_cols=dkv, tn=dkv, out_dtypes=both, name="dsa_v" + tag, **kw)
        qi = _proj(x, w, col_start=c_qi, n_cols=dqi, tn=dqi, rope=(cos_i, sin_i, IDX_DIM),
                   out_dtypes=low, name="dsa_qi" + tag, **kw)
        ki = _proj(x, wt, tn=LANES, rope=(cos_i, sin_i, IDX_DIM), out_dtypes=both,
                   name="dsa_ki" + tag, **kw)
        (wi,) = _proj(x, wt, tn=LANES, name="dsa_wi" + tag, **kw)
        return q, k, v, qi, ki, wi[:, IDX_DIM:IDX_DIM + IDX_HEADS]

    pos_p = jnp.arange(t, dtype=jnp.int32)
    q, k, v, qi, ki, wi = project(xpb, w_in.astype(BF16), w_tail.astype(BF16), PROMPT_TM, False,
                                  *_rope_tables(pos_p, B_HEAD_DIM), *_rope_tables(pos_p, IDX_DIM), "")
    o = _dsa_prompt(qi[0].reshape(b, t, dqi), ki[1][:, :IDX_DIM].reshape(b, t, IDX_DIM),
                    wi.reshape(b, t, IDX_HEADS), q[0].reshape(b, t, dq), k[1].reshape(b, t, dkv),
                    v[1].reshape(b, t, dkv), tq=DSA_TQ)
    x1f, x1b = _proj(o.reshape(b * t, dq), w_o.astype(BF16), tm=PROMPT_LN_TM, tn=D_MODEL,
                     ln=(xpf, ln_g, ln_b), out_dtypes=(F32, BF16), name="dsa_wo")
    bs = xs.shape[0]
    pos_s = jnp.full((bs,), PAST_LEN, jnp.int32)
    qs, ks, vs, qis, kis, wis = project(xs, w_in, w_tail, bs, True, *_rope_tables(pos_s, B_HEAD_DIM),
                                        *_rope_tables(pos_s, IDX_DIM), "_s")
    qs, ks, vs, qis, kis = qs[0], ks[0], vs[0], qis[0], kis[0][:, :IDX_DIM]
    qis3 = qis.reshape(bs, IDX_HEADS, IDX_DIM)
    wis3 = wis.reshape(bs, IDX_HEADS, 1)
    n_pages = page_table.shape[1]
    past_scores = _idx_scores_paged(page_table, qis3, wis3, jnp.swapaxes(cache_kidx, 1, 2),
                                    pages_per_step=DSA_PAGES_PER_STEP)
    new_page = jnp.pad(kis.reshape(bs, IDX_DIM, 1), ((0, 0), (0, 0), (0, PAGE_SIZE - 1)))
    new_scores = _idx_scores_paged(jnp.arange(bs, dtype=jnp.int32).reshape(bs, 1), qis3, wis3, new_page,
                                   pages_per_step=1)
    n_past = n_pages * PAGE_SIZE
    all_scores = jnp.concatenate(
        [past_scores.reshape(bs, n_past), new_scores[:, 0, :1], jnp.full((bs, LANES - 1), -jnp.inf, F32)], axis=1)
    bias_all = _topk_bias(all_scores, min(IDX_TOPK_MAX, (n_past + 1) // 4))
    n_pool = cache_k.shape[0]
    per_head = lambda a: jnp.repeat(a.reshape(bs, B_KV_HEADS, B_HEAD_DIM), B_GROUP, axis=1)
    os_ = _paged_attn(page_table, qs.reshape(bs, B_HEADS, B_HEAD_DIM),
                      cache_k.reshape(n_pool, PAGE_SIZE * B_KV_HEADS, B_HEAD_DIM),
                      cache_v.reshape(n_pool, PAGE_SIZE * B_KV_HEADS, B_HEAD_DIM),
                      per_head(ks), per_head(vs), mode="dsa", pages_per_step=DSA_PAGES_PER_STEP,
                      bias=jnp.repeat(bias_all[:, :n_past], B_KV_HEADS, axis=1).reshape(
                          bs, n_pages, PAGE_SIZE * B_KV_HEADS),
                      bias_new=bias_all[:, n_past:n_past + 1].reshape(bs, 1, 1))
    (xs1,) = _proj(os_.reshape(bs, dq), w_o, tm=bs, tn=D_MODEL, precise=True, ln=(xs, ln_g, ln_b),
                   name="dsa_wo_s")
    state = (k[0].reshape(b, t, B_KV_HEADS, B_HEAD_DIM), v[0].reshape(b, t, B_KV_HEADS, B_HEAD_DIM),
             ki[0][:, :IDX_DIM].reshape(b, t, IDX_DIM),
             ks.reshape(bs, 1, B_KV_HEADS, B_HEAD_DIM), vs.reshape(bs, 1, B_KV_HEADS, B_HEAD_DIM),
             kis.reshape(bs, 1, IDX_DIM))
    return x1f, x1b, xs1, state


def _pool_layer(xpf, xs, state_pool, w_group, scale, ln_g, ln_b, b, t):
    xp3 = xpf.reshape(b, t, D_MODEL)
    sub = _pool_prompt(xp3, w_group.astype(BF16), scale)
    x1f, x1b = _post_norm(xpf, sub.reshape(b * t, D_MODEL), ln_g, ln_b, tm=PROMPT_LN_TM,
                          out_dtypes=(F32, BF16))
    bs = xs.shape[0]
    xs_ext = jnp.concatenate([state_pool.astype(xs.dtype), xs.reshape(bs, 1, D_MODEL)], axis=1)
    sub_s = _pool_sample(xs_ext, w_group, scale)
    (xs1,) = _post_norm(xs, sub_s, ln_g, ln_b, tm=bs, out_dtypes=(F32,))
    return x1f, x1b, xs1, (xp3[:, -POOL_STATE_LEN:], xs_ext[:, -POOL_STATE_LEN:])


def kernel(x_prompt, x_sample, cache_l0_k, cache_l0_v, cache_l1_k, cache_l1_v, cache_l1_kidx, state_l2_pool, cache_l3_k, cache_l3_v, page_table, router_w, router_bias, l0_w_qkv, l0_w_o, l0_lam_q1, l0_lam_k1, l0_lam_q2, l0_lam_k2, l0_subln_g, l0_ln1_g, l0_ln1_b, l0_moe_w_in, l0_moe_w_down, l0_ln2_g, l0_ln2_b, l1_w_in, l1_w_o, l1_ln1_g, l1_ln1_b, l1_moe_w_in, l1_moe_w_down, l1_ln2_g, l1_ln2_b, l2_w_group, l2_scale, l2_ln1_g, l2_ln1_b, l2_moe_w_in, l2_moe_w_down, l2_ln2_g, l2_ln2_b, l3_w_qkv, l3_w_o, l3_lam_q1, l3_lam_k1, l3_lam_q2, l3_lam_k2, l3_subln_g, l3_ln1_g, l3_ln1_b, l3_moe_w_in, l3_moe_w_down, l3_ln2_g, l3_ln2_b):
    b, t, d = x_prompt.shape
    bs = x_sample.shape[0]
    xpf = x_prompt.reshape(b * t, d)
    xpb = xpf.astype(BF16)
    xs = x_sample.reshape(bs, d)
    moe_p = [(l0_moe_w_in, l0_moe_w_down, l0_ln2_g, l0_ln2_b), (l1_moe_w_in, l1_moe_w_down, l1_ln2_g, l1_ln2_b),
             (l2_moe_w_in, l2_moe_w_down, l2_ln2_g, l2_ln2_b), (l3_moe_w_in, l3_moe_w_down, l3_ln2_g, l3_ln2_b)]
    states = []
    for i in range(DEPTH):
        if i == 0:
            xpf, xpb, xs, st = _diff_layer(xpf, xpb, xs, cache_l0_k, cache_l0_v, page_table, l0_w_qkv, l0_w_o,
                                           l0_lam_q1, l0_lam_k1, l0_lam_q2, l0_lam_k2, l0_subln_g,
                                           l0_ln1_g, l0_ln1_b, i, b, t)
        elif i == 1:
            xpf, xpb, xs, st = _dsa_layer(xpf, xpb, xs, cache_l1_k, cache_l1_v, cache_l1_kidx, page_table,
                                          l1_w_in, l1_w_o, l1_ln1_g, l1_ln1_b, b, t)
        elif i == 2:
            xpf, xpb, xs, st = _pool_layer(xpf, xs, state_l2_pool, l2_w_group, l2_scale, l2_ln1_g, l2_ln1_b, b, t)
        else:
            xpf, xpb, xs, st = _diff_layer(xpf, xpb, xs, cache_l3_k, cache_l3_v, page_table, l3_w_qkv, l3_w_o,
                                           l3_lam_q1, l3_lam_k1, l3_lam_q2, l3_lam_k2, l3_subln_g,
                                           l3_ln1_g, l3_ln1_b, i, b, t)
        states.append(st)
        w_in, w_down, ln_g, ln_b = moe_p[i]
        xpf, xpb = _moe_prompt(xpf, xpb, router_w, router_bias, w_in, w_down, ln_g, ln_b)
        xs = _moe_sample(xs, router_w, router_bias, w_in, w_down, ln_g, ln_b)
    l0, l1, l2, l3 = states
    return (xpf.reshape(b, t, d), xs.reshape(bs, 1, d),
            l0[0], l0[1], l0[2], l0[3],
            l1[0], l1[1], l1[2], l1[3], l1[4], l1[5],
            l2[0], l2[1],
            l3[0], l3[1], l3[2], l3[3])
```

```python
import functools
import math

import jax
import jax.numpy as jnp
from jax import lax
from jax.experimental import pallas as pl
from jax.experimental.pallas import tpu as pltpu

F32 = jnp.float32
BF16 = jnp.bfloat16

D_MODEL = 2048
DEPTH = 4
PAST_LEN = 16384
PAGE_SIZE = 128
ROPE_THETA = 10000.0
LN_EPS = 1e-5
A_HEADS = 8
A_HEAD_DIM = D_MODEL // A_HEADS // 2
B_HEADS = 16
B_HEAD_DIM = D_MODEL // B_HEADS
B_KV_HEADS = 4
B_GROUP = B_HEADS // B_KV_HEADS
IDX_HEADS = 16
IDX_DIM = 64
IDX_TOPK_MAX = 256
POOL_WINDOWS = (2, 4, 8, 16)
POOL_GROUPS = 4
POOL_GROUP_DIM = D_MODEL // POOL_GROUPS
POOL_STATE_LEN = max(POOL_WINDOWS) - 1
N_EXPERTS = 16
N_EXPERT_GROUPS = 4
EXPERTS_PER_GROUP = N_EXPERTS // N_EXPERT_GROUPS
D_EXPERT = D_MODEL // 4
DEEPNORM_ALPHA = (2 * DEPTH) ** 0.25

LANES = 128
SUBLANES = 8
VMEM_LIMIT_BYTES = 56 * 1024 * 1024

NEG_BIG = -1e30
HIGHEST = lax.Precision.HIGHEST


def _params(*sem):
    return pltpu.CompilerParams(dimension_semantics=sem, vmem_limit_bytes=VMEM_LIMIT_BYTES)


def _split_bf16(a):
    hi = a.astype(BF16)
    return hi, (a - hi.astype(F32)).astype(BF16)


def _dot(a, b, precise):
    if precise:
        a = a.astype(F32)
        m = a.shape[0]
        a_hi_b = a.astype(BF16)
        a_hi = a_hi_b.astype(F32)
        a_both = jnp.concatenate([a_hi, a - a_hi], axis=0).astype(BF16)
        b_hi, b_lo = _split_bf16(b.astype(F32))
        r = jnp.dot(a_both, b_hi, preferred_element_type=F32)
        return r[:m] + r[m:] + jnp.dot(a_hi_b, b_lo, preferred_element_type=F32)
    return jnp.dot(a.astype(BF16), b.astype(BF16), preferred_element_type=F32)


def _dot_nt(a, b, precise):
    dims = (((1,), (1,)), ((), ()))
    if precise:
        return lax.dot_general(a.astype(F32), b.astype(F32), dims, precision=HIGHEST,
                               preferred_element_type=F32)
    return lax.dot_general(a.astype(BF16), b.astype(BF16), dims, preferred_element_type=F32)


def _layer_norm_rows(r, g, b):
    mu = jnp.mean(r, axis=-1, keepdims=True)
    c = r - mu
    var = jnp.mean(c * c, axis=-1, keepdims=True)
    return c * lax.rsqrt(var + LN_EPS) * g + b


def _rope_tables(pos, head_dim):
    half = head_dim // 2
    inv_freq = ROPE_THETA ** (-jnp.arange(half, dtype=F32) * 2.0 / head_dim)
    ang = pos.astype(F32)[:, None] * inv_freq[None, :]
    cos, sin = jnp.cos(ang), jnp.sin(ang)
    reps = LANES // head_dim
    cos_t = jnp.tile(jnp.concatenate([cos, cos], axis=-1), (1, reps))
    sin_t = jnp.tile(jnp.concatenate([-sin, sin], axis=-1), (1, reps))
    return cos_t, sin_t


def _rope_lanes(x, cos_t, sin_t, head_dim):
    half = head_dim // 2
    if head_dim == LANES:
        partner = pltpu.roll(x, half, axis=1)
    else:
        lane = lax.broadcasted_iota(jnp.int32, x.shape, 1)
        first = (lane % head_dim) < half
        partner = jnp.where(first, pltpu.roll(x, LANES - half, axis=1), pltpu.roll(x, half, axis=1))
    return x * cos_t + partner * sin_t


def _proj_kernel(*refs, precise, rope_dim, has_ln, n_out, out_scale):
    it = iter(refs)
    x_ref, w_ref = next(it), next(it)
    cos_ref = sin_ref = res_ref = g_ref = b_ref = None
    if rope_dim:
        cos_ref, sin_ref = next(it), next(it)
    if has_ln:
        res_ref, g_ref, b_ref = next(it), next(it), next(it)
    outs = [next(it) for _ in range(n_out)]
    acc = _dot(x_ref[...], w_ref[...], precise)
    if rope_dim:
        cos_t, sin_t = cos_ref[...], sin_ref[...]
        pieces = [_rope_lanes(acc[:, c * LANES:(c + 1) * LANES], cos_t, sin_t, rope_dim)
                  for c in range(acc.shape[1] // LANES)]
        acc = jnp.concatenate(pieces, axis=1) if len(pieces) > 1 else pieces[0]
    if has_ln:
        acc = _layer_norm_rows(DEEPNORM_ALPHA * res_ref[...] + acc, g_ref[...], b_ref[...])
    if out_scale != 1.0:
        acc = acc * out_scale
    for o in outs:
        o[...] = acc.astype(o.dtype)


def _proj(x, w, *, col_start=0, n_cols=None, tm, tn, precise=False, rope=None, ln=None,
          out_dtypes=(F32,), out_scale=1.0, name="proj"):
    m, k = x.shape
    tm = min(tm, m)
    n_cols = w.shape[1] - col_start if n_cols is None else n_cols
    assert m % tm == 0 and n_cols % tn == 0 and col_start % tn == 0
    cb = col_start // tn
    grid = (n_cols // tn, m // tm)
    in_specs = [pl.BlockSpec((tm, k), lambda j, i: (i, 0)),
                pl.BlockSpec((k, tn), lambda j, i: (0, cb + j))]
    args = [x, w]
    rope_dim = 0
    if rope is not None:
        cos_t, sin_t, rope_dim = rope
        period = cos_t.shape[0] // tm
        in_specs += [pl.BlockSpec((tm, LANES), lambda j, i: (i % period, 0))] * 2
        args += [cos_t, sin_t]
    if ln is not None:
        assert tn == n_cols
        res, g, b = ln
        in_specs += [pl.BlockSpec((tm, tn), lambda j, i: (i, 0)),
                     pl.BlockSpec((1, tn), lambda j, i: (0, 0)),
                     pl.BlockSpec((1, tn), lambda j, i: (0, 0))]
        args += [res, g.reshape(1, -1), b.reshape(1, -1)]
    out_shape = [jax.ShapeDtypeStruct((m, n_cols), dt) for dt in out_dtypes]
    out_specs = [pl.BlockSpec((tm, tn), lambda j, i: (i, j)) for _ in out_dtypes]
    kern = functools.partial(_proj_kernel, precise=precise, rope_dim=rope_dim, has_ln=ln is not None,
                             n_out=len(out_dtypes), out_scale=out_scale)
    res = pl.pallas_call(kern, out_shape=out_shape, grid=grid, in_specs=in_specs, out_specs=out_specs,
                         compiler_params=_params("arbitrary", "arbitrary"), name=name)(*args)
    return res


def _lam_scalar(lam_ref, lam_init):
    v = lam_ref[...]
    a = jnp.sum(v[0:1] * v[1:2], axis=1, keepdims=True)
    b = jnp.sum(v[2:3] * v[3:4], axis=1, keepdims=True)
    return jnp.exp(a) - jnp.exp(b) + lam_init


def _subln(o, g, lam_init):
    ms = jnp.mean(o * o, axis=-1, keepdims=True)
    return o * lax.rsqrt(ms + LN_EPS) * g * (1.0 - lam_init)


def _diff_attn_kernel(lam_ref, g_ref, q_ref, k_ref, v_ref, o_ref, *, tq, lam_init, n_blocks):
    i = pl.program_id(2)
    hd = A_HEAD_DIM
    lam = _lam_scalar(lam_ref, lam_init)
    row = lax.broadcasted_iota(jnp.int32, (tq, tq), 0)
    col = lax.broadcasted_iota(jnp.int32, (tq, tq), 1)
    on_or_below_diag = col <= row

    def branch(c):
        n_left = c * tq
        q = q_ref[0]
        streams = []
        for s_ in range(2):
            lanes = slice(s_ * hd, (s_ + 1) * hd)
            qs = q[:, lanes]
            sd = _dot_nt(qs, k_ref[0, n_left:n_left + tq, lanes], False)
            sd = jnp.where(on_or_below_diag, sd, NEG_BIG)
            m = jnp.max(sd, axis=1, keepdims=True)
            if c > 0:
                sl = _dot_nt(qs, k_ref[0, 0:n_left, lanes], False)
                m = jnp.maximum(m, jnp.max(sl, axis=1, keepdims=True))
            pd = jnp.exp2(sd - m)
            l = jnp.sum(pd, axis=1, keepdims=True)
            o = _dot(pd, v_ref[0, n_left:n_left + tq, :], False)
            if c > 0:
                p_left = jnp.exp2(sl - m)
                l = l + jnp.sum(p_left, axis=1, keepdims=True)
                o = o + _dot(p_left, v_ref[0, 0:n_left, :], False)
            streams.append(o / l)
        o = streams[0] - lam * streams[1]
        o_ref[0] = _subln(o, g_ref[...], lam_init).astype(o_ref.dtype)

    for c in range(n_blocks):
        pl.when(i == c)(functools.partial(branch, c))


LOG2_E = 1.4426950408889634


def _diff_attn_prompt(q, k, v, lam_vecs, subln_g, lam_init, *, tq):
    b, t, _ = q.shape
    w = 2 * A_HEAD_DIM
    grid = (b, A_HEADS, t // tq)
    kern = functools.partial(_diff_attn_kernel, tq=tq, lam_init=lam_init, n_blocks=t // tq)
    return pl.pallas_call(
        kern, out_shape=jax.ShapeDtypeStruct((b, t, A_HEADS * w), BF16), grid=grid,
        in_specs=[pl.BlockSpec((4, A_HEAD_DIM), lambda bi, h, i: (0, 0)),
                  pl.BlockSpec((1, w), lambda bi, h, i: (0, 0)),
                  pl.BlockSpec((1, tq, w), lambda bi, h, i: (bi, i, h)),
                  pl.BlockSpec((1, t, w), lambda bi, h, i: (bi, 0, h)),
                  pl.BlockSpec((1, t, w), lambda bi, h, i: (bi, 0, h))],
        out_specs=pl.BlockSpec((1, tq, w), lambda bi, h, i: (bi, i, h)),
        compiler_params=_params("arbitrary", "arbitrary", "arbitrary"), name="diff_attn_prompt",
    )(lam_vecs, subln_g.reshape(1, w), q, k, v)


def _paged_attn_kernel(pt_ref, *refs, pages_per_step, scale, mode, lam_init):
    g_pages = pages_per_step
    it = iter(refs)
    q_ref = next(it)
    bias_ref = bias_new_ref = lam_ref = g_ref = None
    if mode == "dsa":
        bias_ref, bias_new_ref = next(it), next(it)
    else:
        lam_ref, g_ref = next(it), next(it)
    k_refs = [next(it) for _ in range(g_pages)]
    v_refs = [next(it) for _ in range(g_pages)]
    knew_ref, vnew_ref = next(it), next(it)
    o_ref = next(it)
    m_ref, l_ref, acc_ref = next(it), next(it), next(it)
    j = pl.program_id(1)
    n_steps = pl.num_programs(1)

    @pl.when(j == 0)
    def _():
        m_ref[...] = jnp.full(m_ref.shape, NEG_BIG, F32)
        l_ref[...] = jnp.zeros(l_ref.shape, F32)
        acc_ref[...] = jnp.zeros(acc_ref.shape, F32)

    q = q_ref[0]
    n_rows = q.shape[0]
    if mode == "diff":
        lanes = PAGE_SIZE * A_HEADS
        wanted = lambda r, lane: (lane % A_HEADS) == (r % A_HEADS)
    else:
        lanes = PAGE_SIZE * B_KV_HEADS
        wanted = lambda r, lane: (lane % B_KV_HEADS) == (r // B_GROUP)
    ri = lax.broadcasted_iota(jnp.int32, (n_rows, lanes), 0)
    li = lax.broadcasted_iota(jnp.int32, (n_rows, lanes), 1)
    own = wanted(ri, li)

    def update(scores, values):
        m_old = m_ref[...]
        m_new = m_old
        for s in scores:
            m_new = jnp.maximum(m_new, jnp.max(s, axis=1, keepdims=True))
        alpha = jnp.exp(m_old - m_new)
        l = alpha * l_ref[...]
        acc = alpha * acc_ref[...]
        for s, v_rows in zip(scores, values):
            p = jnp.exp(s - m_new)
            l = l + jnp.sum(p, axis=1, keepdims=True)
            acc = acc + (p * v_rows if s.shape[1] == 1 else _dot(p, v_rows, False))
        l_ref[...] = l
        acc_ref[...] = acc
        m_ref[...] = m_new

    scores = []
    for g in range(g_pages):
        if mode == "diff":
            half = n_rows // 2
            s = jnp.concatenate(
                [_dot_nt(q[c * half:(c + 1) * half], k_refs[g][0, pl.ds(c, lanes, stride=2), :], False)
                 for c in range(2)], axis=0)
        else:
            s = _dot_nt(q, k_refs[g][0], False)
        s = jnp.where(own, s * scale, NEG_BIG)
        if mode == "dsa":
            s = s + bias_ref[0, pl.ds(j * g_pages + g, 1), :]
        scores.append(s)
    update(scores, [v_refs[g][0] for g in range(g_pages)])

    @pl.when(j == n_steps - 1)
    def _():
        s_new = jnp.sum(q * knew_ref[0], axis=1, keepdims=True) * scale
        if mode == "dsa":
            s_new = s_new + bias_new_ref[0]
        update([s_new], [vnew_ref[0]])
        o = acc_ref[...] / l_ref[...]
        if mode == "diff":
            half = n_rows // 2
            o = _subln(o[:half] - _lam_scalar(lam_ref, lam_init) * o[half:], g_ref[...], lam_init)
        o_ref[0] = o


def _paged_attn(page_table, q, cache_k, cache_v, k_new, v_new, *, mode, pages_per_step,
                bias=None, bias_new=None, lam_vecs=None, subln_g=None, lam_init=0.0):
    bs, n_rows, d = q.shape
    n_pages = page_table.shape[1]
    g = pages_per_step
    assert n_pages % g == 0
    grid = (bs, n_pages // g)
    e = cache_v.shape[2]
    in_specs = [pl.BlockSpec((1, n_rows, d), lambda b, j, pt: (b, 0, 0))]
    args = [q]
    if mode == "dsa":
        in_specs += [pl.BlockSpec((1, n_pages, bias.shape[2]), lambda b, j, pt: (b, 0, 0)),
                     pl.BlockSpec((1, 1, 1), lambda b, j, pt: (b, 0, 0))]
        args += [bias, bias_new]
        out_rows = n_rows
    else:
        in_specs += [pl.BlockSpec((4, A_HEAD_DIM), lambda b, j, pt: (0, 0)),
                     pl.BlockSpec((1, e), lambda b, j, pt: (0, 0))]
        args += [lam_vecs, subln_g.reshape(1, -1)]
        out_rows = n_rows // 2

    def page_spec(arr, gi):
        return pl.BlockSpec((1,) + arr.shape[1:], lambda b, j, pt: (pt[b, j * g + gi], 0, 0))

    in_specs += [page_spec(cache_k, gi) for gi in range(g)] + [page_spec(cache_v, gi) for gi in range(g)]
    args += [cache_k] * g + [cache_v] * g
    in_specs += [pl.BlockSpec((1, n_rows, d), lambda b, j, pt: (b, 0, 0)),
                 pl.BlockSpec((1, n_rows, e), lambda b, j, pt: (b, 0, 0))]
    args += [k_new, v_new]
    kern = functools.partial(_paged_attn_kernel, pages_per_step=g, scale=d ** -0.5, mode=mode,
                             lam_init=lam_init)
    grid_spec = pltpu.PrefetchScalarGridSpec(
        num_scalar_prefetch=1, grid=grid, in_specs=in_specs,
        out_specs=pl.BlockSpec((1, out_rows, e), lambda b, j, pt: (b, 0, 0)),
        scratch_shapes=[pltpu.VMEM((n_rows, 1), F32), pltpu.VMEM((n_rows, 1), F32),
                        pltpu.VMEM((n_rows, e), F32)])
    return pl.pallas_call(kern, out_shape=jax.ShapeDtypeStruct((bs, out_rows, e), F32),
                          grid_spec=grid_spec, compiler_params=_params("arbitrary", "arbitrary"),
                          name="paged_attn_" + mode)(page_table, *args)


INT32_MIN = -2 ** 31
KEY_OF_NEG_INF = (-8388608) ^ 0x7FFFFFFF


def _topk_bias_into(bias_ref, scores, k_sel):
    r, n = scores.shape
    scores = jnp.where(scores == 0.0, 0.0, scores)
    bits = pltpu.bitcast(scores, jnp.int32)
    key = jnp.where(bits < 0, bits ^ 0x7FFFFFFF, bits)
    kf = float(k_sel)

    def step(t, ans):
        cand = ans + jnp.left_shift(jnp.int32(1), 31 - t)
        cnt = jnp.sum(jnp.where(key >= cand, 1.0, 0.0), axis=1, keepdims=True)
        return jnp.where(cnt >= kf, cand, ans)

    thr = lax.fori_loop(0, 32, step, jnp.full((r, 1), INT32_MIN, jnp.int32), unroll=4)
    gt = key > thr
    eq = key == thr
    cnt_gt = jnp.sum(jnp.where(gt, 1.0, 0.0), axis=1, keepdims=True)
    cnt_eq = jnp.sum(jnp.where(eq, 1.0, 0.0), axis=1, keepdims=True)
    bias_ref[...] = jnp.where(key >= thr, 0.0, NEG_BIG)
    tie = jnp.max(jnp.where((cnt_gt + cnt_eq > kf) & (thr > KEY_OF_NEG_INF), 1.0, 0.0))

    @pl.when(tie > 0.0)
    def _():
        need = kf - cnt_gt
        ri = lax.broadcasted_iota(jnp.int32, (LANES, LANES), 0)
        ci = lax.broadcasted_iota(jnp.int32, (LANES, LANES), 1)
        tri = jnp.where(ri < ci, 1.0, 0.0).astype(BF16)
        run = jnp.zeros((r, 1), F32)
        for c in range(n // LANES):
            sl = slice(c * LANES, (c + 1) * LANES)
            eq_c = jnp.where(eq[:, sl], 1.0, 0.0)
            before = jnp.dot(eq_c.astype(BF16), tri, preferred_element_type=F32) + run
            take = gt[:, sl] | (eq[:, sl] & (before < need))
            bias_ref[:, sl] = jnp.where(take, 0.0, NEG_BIG)
            run = run + jnp.sum(eq_c, axis=1, keepdims=True)


def _dsa_prompt_kernel(qi_ref, kibd_ref, wi_ref, q_ref, k_ref, v_ref, o_ref, bias_ref, *, tq, k_sel,
                       width_step):
    i = pl.program_id(1)
    t = k_ref.shape[1]

    def branch(n):
        wi = wi_ref[0]
        qi = qi_ref[0]
        score = jnp.zeros((tq, n), F32)
        for hp in range(IDX_HEADS // 2):
            q_pair = qi[:, hp * LANES:(hp + 1) * LANES]
            for c in range(2):
                d = _dot_nt(q_pair, kibd_ref[0, c * t:c * t + n, :], False)
                score = score + wi[:, 2 * hp + c:2 * hp + c + 1] * jnp.maximum(d, 0.0)
        row = i * tq + lax.broadcasted_iota(jnp.int32, (tq, n), 0)
        col = lax.broadcasted_iota(jnp.int32, (tq, n), 1)
        causal = col <= row
        bias_view = bias_ref.at[:, pl.ds(0, n)]
        _topk_bias_into(bias_view, jnp.where(causal, score, -jnp.inf), k_sel)
        bias = jnp.where(causal, bias_view[...], NEG_BIG)
        bias4 = jnp.concatenate([bias] * B_GROUP, axis=0)
        q = q_ref[0]
        outs = []
        for kv in range(B_KV_HEADS):
            qg = jnp.concatenate([q[:, (kv * B_GROUP + r) * B_HEAD_DIM:(kv * B_GROUP + r + 1) * B_HEAD_DIM]
                                  for r in range(B_GROUP)], axis=0)
            kg = k_ref[0, 0:n, kv * B_HEAD_DIM:(kv + 1) * B_HEAD_DIM]
            vg = v_ref[0, 0:n, kv * B_HEAD_DIM:(kv + 1) * B_HEAD_DIM]
            s = _dot_nt(qg, kg, False) + bias4
            p = jnp.exp2(s - jnp.max(s, axis=1, keepdims=True))
            l = jnp.sum(p, axis=1, keepdims=True)
            og = _dot(p, vg, False) / l
            outs += [og[r * tq:(r + 1) * tq] for r in range(B_GROUP)]
        o_ref[0] = jnp.concatenate(outs, axis=1).astype(o_ref.dtype)

    for cls in range(t // width_step):
        pl.when((i * tq + tq - 1) // width_step == cls)(functools.partial(branch, (cls + 1) * width_step))


def _dsa_prompt(qi, ki, wi, q, k, v, *, tq):
    b, t, _ = q.shape
    k_sel = min(IDX_TOPK_MAX, t // 4)
    zeros = jnp.zeros_like(ki)
    kibd = jnp.concatenate([jnp.concatenate([ki, zeros], axis=-1),
                            jnp.concatenate([zeros, ki], axis=-1)], axis=1)
    kvw = B_KV_HEADS * B_HEAD_DIM
    kern = functools.partial(_dsa_prompt_kernel, tq=tq, k_sel=k_sel, width_step=min(DSA_WIDTH_STEP, t))
    return pl.pallas_call(
        kern, out_shape=jax.ShapeDtypeStruct((b, t, D_MODEL), BF16), grid=(b, t // tq),
        in_specs=[pl.BlockSpec((1, tq, IDX_HEADS * IDX_DIM), lambda bi, i: (bi, i, 0)),
                  pl.BlockSpec((1, 2 * t, LANES), lambda bi, i: (bi, 0, 0)),
                  pl.BlockSpec((1, tq, IDX_HEADS), lambda bi, i: (bi, i, 0)),
                  pl.BlockSpec((1, tq, D_MODEL), lambda bi, i: (bi, i, 0)),
                  pl.BlockSpec((1, t, kvw), lambda bi, i: (bi, 0, 0)),
                  pl.BlockSpec((1, t, kvw), lambda bi, i: (bi, 0, 0))],
        out_specs=pl.BlockSpec((1, tq, D_MODEL), lambda bi, i: (bi, i, 0)),
        scratch_shapes=[pltpu.VMEM((tq, t), F32)],
        compiler_params=_params("arbitrary", "arbitrary"), name="dsa_prompt",
    )(qi, kibd, wi, q, k, v)


def _idx_scores_kernel(pt_ref, qi_ref, w_ref, *refs, pages_per_step):
    page_refs, o_ref = refs[:pages_per_step], refs[pages_per_step]
    qi = qi_ref[0]
    w = w_ref[0]
    rows = []
    for g in range(pages_per_step):
        d = _dot(qi, page_refs[g][0], True)
        rows.append(jnp.sum(w * jnp.maximum(d, 0.0), axis=0, keepdims=True))
    o_ref[0] = jnp.concatenate(rows, axis=0)


def _idx_scores_paged(page_table, qi, wi, cache_kidx, *, pages_per_step):
    bs = qi.shape[0]
    n_pages = page_table.shape[1]
    g = pages_per_step

    def page_spec(gi):
        return pl.BlockSpec((1, IDX_DIM, PAGE_SIZE), lambda b, j, pt: (pt[b, j * g + gi], 0, 0))

    grid_spec = pltpu.PrefetchScalarGridSpec(
        num_scalar_prefetch=1, grid=(bs, n_pages // g),
        in_specs=[pl.BlockSpec((1, IDX_HEADS, IDX_DIM), lambda b, j, pt: (b, 0, 0)),
                  pl.BlockSpec((1, IDX_HEADS, 1), lambda b, j, pt: (b, 0, 0))]
        + [page_spec(gi) for gi in range(g)],
        out_specs=pl.BlockSpec((1, g, PAGE_SIZE), lambda b, j, pt: (b, j, 0)))
    kern = functools.partial(_idx_scores_kernel, pages_per_step=g)
    return pl.pallas_call(kern, out_shape=jax.ShapeDtypeStruct((bs, n_pages, PAGE_SIZE), F32),
                          grid_spec=grid_spec, compiler_params=_params("arbitrary", "arbitrary"),
                          name="dsa_idx_scores")(page_table, qi, wi, *([cache_kidx] * g))


def _topk_bias_kernel(s_ref, o_ref, *, k_sel):
    _topk_bias_into(o_ref, s_ref[...], k_sel)


def _topk_bias(scores, k_sel):
    return pl.pallas_call(functools.partial(_topk_bias_kernel, k_sel=k_sel),
                          out_shape=jax.ShapeDtypeStruct(scores.shape, F32),
                          compiler_params=pltpu.CompilerParams(vmem_limit_bytes=VMEM_LIMIT_BYTES),
                          name="topk_bias")(scores)


def _pool_prompt_kernel(x_ref, w_ref, sc_ref, o_ref, pad_ref, *, window, rows):
    t = x_ref.shape[1]
    halo = 2 * SUBLANES
    pad_ref[0:halo, :] = jnp.zeros((halo, pad_ref.shape[1]), F32)
    pad_ref[halo:, :] = x_ref[0]
    w = w_ref[0]
    sc = sc_ref[...]

    def body(c, carry):
        r0 = pl.multiple_of(c * rows, rows)
        xh = pad_ref[pl.ds(r0, rows + halo), :]
        x = xh[halo:, :]
        win = x
        for jj in range(1, window):
            win = win + xh[halo - jj:halo - jj + rows, :]
        pos = r0 + lax.broadcasted_iota(jnp.int32, (rows, 1), 0)
        count = jnp.minimum(window, pos + 1).astype(F32)
        mixed = win / count - x
        o_ref[0, pl.ds(r0, rows), :] = _dot(mixed, w, False) * sc
        return carry

    lax.fori_loop(0, t // rows, body, 0)


def _pool_prompt(x, w_group_bf16, scale):
    b, t, d = x.shape
    gd = POOL_GROUP_DIM
    outs = []
    for g, window in enumerate(POOL_WINDOWS):
        kern = functools.partial(_pool_prompt_kernel, window=window, rows=256)
        outs.append(pl.pallas_call(
            kern, out_shape=jax.ShapeDtypeStruct((b, t, gd), F32), grid=(b,),
            in_specs=[pl.BlockSpec((1, t, gd), lambda bi, g=g: (bi, 0, g)),
                      pl.BlockSpec((1, gd, gd), lambda bi, g=g: (g, 0, 0)),
                      pl.BlockSpec((1, gd), lambda bi, g=g: (0, g))],
            out_specs=pl.BlockSpec((1, t, gd), lambda bi: (bi, 0, 0)),
            scratch_shapes=[pltpu.VMEM((t + 2 * SUBLANES, gd), F32)],
            compiler_params=_params("arbitrary"), name="pool_prompt_w%d" % window,
        )(x, w_group_bf16, scale.reshape(1, d)))
    return jnp.concatenate(outs, axis=-1)


def _pool_sample_kernel(x_ref, w_ref, sc_ref, o_ref):
    g = pl.program_id(0)
    window = jnp.left_shift(jnp.int32(2), g)
    x = x_ref[...]
    n = x.shape[1]
    rowi = lax.broadcasted_iota(jnp.int32, x.shape, 1)
    win = jnp.sum(jnp.where(rowi >= n - window, x, 0.0), axis=1)
    mixed = win / window.astype(F32) - x[:, n - 1, :]
    o_ref[...] = _dot(mixed, w_ref[0], True) * sc_ref[...]


def _pool_sample(x_ext, w_group, scale):
    b, n, d = x_ext.shape
    gd = POOL_GROUP_DIM
    return pl.pallas_call(
        _pool_sample_kernel, out_shape=jax.ShapeDtypeStruct((b, d), F32), grid=(POOL_GROUPS,),
        in_specs=[pl.BlockSpec((b, n, gd), lambda g: (0, 0, g)),
                  pl.BlockSpec((1, gd, gd), lambda g: (g, 0, 0)),
                  pl.BlockSpec((1, gd), lambda g: (0, g))],
        out_specs=pl.BlockSpec((b, gd), lambda g: (0, g)),
        compiler_params=_params("arbitrary"), name="pool_sample",
    )(x_ext, w_group, scale.reshape(1, d))


def _router_kernel(x_ref, rw_ref, rb_ref, g_ref, *, full_f32):
    tm = x_ref.shape[0]
    if full_f32:
        logits = jnp.dot(x_ref[...], rw_ref[...], precision=HIGHEST, preferred_element_type=F32)
    else:
        x_hi, x_lo = _split_bf16(x_ref[...])
        w_hi, w_lo = _split_bf16(rw_ref[...])
        logits = (jnp.dot(x_hi, w_hi, preferred_element_type=F32)
                  + jnp.dot(x_lo, w_hi, preferred_element_type=F32)
                  + jnp.dot(x_hi, w_lo, preferred_element_type=F32))
    lane = lax.broadcasted_iota(jnp.int32, logits.shape, 1)
    logits = jnp.where(lane < N_EXPERTS, logits, NEG_BIG)
    e = jnp.exp(logits - jnp.max(logits, axis=1, keepdims=True))
    aff_t = (e / jnp.sum(e, axis=1, keepdims=True)).T
    bias = rb_ref[...]
    aff = [aff_t[x:x + 1, :] for x in range(N_EXPERTS)]
    sel = [aff[x] + bias[x:x + 1, :] for x in range(N_EXPERTS)]
    gsz = EXPERTS_PER_GROUP
    best_score = best_group = None
    for g in range(N_EXPERT_GROUPS):
        a, b, c, d = sel[g * gsz:(g + 1) * gsz]
        hi1, lo1, hi2, lo2 = jnp.maximum(a, b), jnp.minimum(a, b), jnp.maximum(c, d), jnp.minimum(c, d)
        score = jnp.maximum(hi1, hi2) + jnp.maximum(jnp.minimum(hi1, hi2), jnp.maximum(lo1, lo2))
        if g == 0:
            best_score, best_group = score, jnp.zeros(score.shape, jnp.int32)
        else:
            better = score > best_score
            best_group = jnp.where(better, g, best_group)
            best_score = jnp.where(better, score, best_score)
    chosen = []
    for x in range(N_EXPERTS):
        g = x // gsz
        rank = jnp.zeros((1, tm), F32)
        for y in range(g * gsz, (g + 1) * gsz):
            if y == x:
                continue
            ahead = (sel[y] > sel[x]) | ((sel[y] == sel[x]) & (y < x))
            rank = rank + jnp.where(ahead, 1.0, 0.0)
        chosen.append((best_group == g) & (rank < 2.0))
    top_sum = jnp.zeros((1, tm), F32)
    for x in range(N_EXPERTS):
        top_sum = top_sum + jnp.where(chosen[x], aff[x], 0.0)
    rows = [jnp.where(chosen[x], aff[x] / top_sum, 0.0) for x in range(N_EXPERTS)]
    rows += [jnp.where(chosen[x], 1.0, 0.0) for x in range(N_EXPERTS)]
    rows.append(jnp.zeros((LANES - 2 * N_EXPERTS, tm), F32))
    g_ref[...] = jnp.concatenate(rows, axis=0).T


def _router(x, router_w, router_bias, *, tm, full_f32):
    m, d = x.shape
    tm = min(tm, m)
    rw = jnp.pad(router_w, ((0, 0), (0, LANES - N_EXPERTS)))
    return pl.pallas_call(
        functools.partial(_router_kernel, full_f32=full_f32),
        out_shape=jax.ShapeDtypeStruct((m, LANES), F32), grid=(m // tm,),
        in_specs=[pl.BlockSpec((tm, d), lambda i: (i, 0)),
                  pl.BlockSpec((d, LANES), lambda i: (0, 0)),
                  pl.BlockSpec((N_EXPERTS, 1), lambda i: (0, 0))],
        out_specs=pl.BlockSpec((tm, LANES), lambda i: (i, 0)),
        compiler_params=_params("arbitrary"), name="router",
    )(x, rw, router_bias.reshape(N_EXPERTS, 1))


def _moe_dense_kernel(ids_ref, na_ref, x_ref, g_ref, win_ref, wdn_ref, res_ref, lg_ref, lb_ref, *refs,
                      precise):
    out_refs, acc_ref = refs[:-1], refs[-1]
    e = pl.program_id(1)

    @pl.when(e == 0)
    def _():
        acc_ref[...] = jnp.zeros(acc_ref.shape, F32)

    @pl.when(e < na_ref[0])
    def _():
        h = _dot(x_ref[...], win_ref[0], precise)
        hg, hu = h[:, :D_EXPERT], h[:, D_EXPERT:]
        gates = g_ref[...]
        lane = lax.broadcasted_iota(jnp.int32, gates.shape, 1)
        gate = jnp.sum(jnp.where(lane == ids_ref[e], gates, 0.0), axis=1, keepdims=True)
        a = hg * jax.nn.sigmoid(hg) * hu * gate
        acc_ref[...] += _dot(a, wdn_ref[0], precise)

    @pl.when(e == pl.num_programs(1) - 1)
    def _():
        y = _layer_norm_rows(DEEPNORM_ALPHA * res_ref[...] + acc_ref[...], lg_ref[...], lb_ref[...])
        for o in out_refs:
            o[...] = y.astype(o.dtype)


def _moe_dense(x, routed, w_in, w_down, res, ln_g, ln_b, *, tm, precise, out_dtypes):
    m, d = x.shape
    tm = min(tm, m)
    active = jnp.any(routed[:, N_EXPERTS:2 * N_EXPERTS] > 0.5, axis=0)
    order = jnp.argsort(jnp.logical_not(active), stable=True).astype(jnp.int32)
    n_act = jnp.sum(active.astype(jnp.int32))
    ids = jnp.where(jnp.arange(N_EXPERTS) < n_act, order, order[jnp.maximum(n_act - 1, 0)])
    kern = functools.partial(_moe_dense_kernel, precise=precise)
    grid_spec = pltpu.PrefetchScalarGridSpec(
        num_scalar_prefetch=2, grid=(m // tm, N_EXPERTS),
        in_specs=[pl.BlockSpec((tm, d), lambda i, e, ids, na: (i, 0)),
                  pl.BlockSpec((tm, LANES), lambda i, e, ids, na: (i, 0)),
                  pl.BlockSpec((1, d, 2 * D_EXPERT), lambda i, e, ids, na: (ids[e], 0, 0)),
                  pl.BlockSpec((1, D_EXPERT, d), lambda i, e, ids, na: (ids[e], 0, 0)),
                  pl.BlockSpec((tm, d), lambda i, e, ids, na: (i, 0)),
                  pl.BlockSpec((1, d), lambda i, e, ids, na: (0, 0)),
                  pl.BlockSpec((1, d), lambda i, e, ids, na: (0, 0))],
        out_specs=[pl.BlockSpec((tm, d), lambda i, e, ids, na: (i, 0)) for _ in out_dtypes],
        scratch_shapes=[pltpu.VMEM((tm, d), F32)])
    return pl.pallas_call(
        kern, out_shape=[jax.ShapeDtypeStruct((m, d), dt) for dt in out_dtypes], grid_spec=grid_spec,
        compiler_params=_params("arbitrary", "arbitrary"), name="moe_dense",
    )(ids, n_act.reshape(1), x, routed, w_in, w_down, res, ln_g.reshape(1, d), ln_b.reshape(1, d))


def _moe_plan(gates_and_mask, tile):
    gates = gates_and_mask[:, :N_EXPERTS]
    chosen = gates_and_mask[:, N_EXPERTS:2 * N_EXPERTS] > 0.5
    m = gates.shape[0]
    c = chosen.astype(jnp.int32)
    rank = jnp.cumsum(c, axis=0) - c
    counts = jnp.sum(c, axis=0)
    padded = ((counts + tile - 1) // tile) * tile
    ends = jnp.cumsum(padded)
    slot = (ends - padded)[None, :] + rank
    slot_a = jnp.min(jnp.where(chosen, slot, jnp.int32(2 ** 30)), axis=1)
    slot_b = jnp.max(jnp.where(chosen, slot, jnp.int32(-1)), axis=1)
    gate_a = jnp.sum(jnp.where(chosen & (slot == slot_a[:, None]), gates, 0.0), axis=1)
    gate_b = jnp.sum(jnp.where(chosen & (slot == slot_b[:, None]), gates, 0.0), axis=1)
    n_tiles = 2 * m // tile + N_EXPERTS
    starts = jnp.arange(n_tiles, dtype=jnp.int32) * tile
    tile_expert = jnp.minimum(jnp.sum((starts[:, None] >= ends[None, :]).astype(jnp.int32), axis=1),
                              N_EXPERTS - 1)
    n_used = (ends[-1] // tile).astype(jnp.int32).reshape(1)
    return (jnp.stack([slot_a, slot_b], axis=1), jnp.stack([gate_a, gate_b], axis=1),
            tile_expert, n_used, n_tiles, ends.astype(jnp.int32), padded.astype(jnp.int32))


def _row_copy(src, src_row, dst, dst_row, sem):
    return pltpu.make_async_copy(src.at[pl.ds(src_row, 1)], dst.at[pl.ds(dst_row, 1)], sem)


def _moe_dispatch_kernel(ends_ref, padded_ref, slots_ref, x_ref, xs_ref, zero_ref, stage_ref, zero_sem, sems,
                         *, expert_tile):
    i = pl.program_id(0)
    t = slots_ref.shape[2] // 2

    @pl.when(i == 0)
    def _():
        zero_ref[...] = jnp.zeros(zero_ref.shape, zero_ref.dtype)

        def clear_tile(first_row):
            first = pl.multiple_of(first_row, expert_tile)
            return pltpu.make_async_copy(zero_ref, xs_ref.at[pl.ds(first, expert_tile)], zero_sem)

        def clear_all(act):
            for e in range(N_EXPERTS):
                pl.when(padded_ref[e] > 0)(lambda e=e: act(clear_tile(ends_ref[e] - expert_tile)))
            for j in range(xs_ref.shape[0] // expert_tile):
                pl.when(j * expert_tile >= ends_ref[N_EXPERTS - 1])(
                    lambda j=j: act(clear_tile(j * expert_tile)))

        clear_all(lambda cp: cp.start())
        clear_all(lambda cp: cp.wait())

    cur = lax.rem(i, 2)
    stage_ref[cur] = x_ref[...]

    def start(r, carry):
        for k in range(2):
            _row_copy(stage_ref.at[cur], r, xs_ref, slots_ref[0, 0, k * t + r], sems.at[cur]).start()
        return carry

    lax.fori_loop(0, t, start, 0, unroll=4)

    def drain(slot):
        rows = xs_ref.at[pl.ds(0, 2 * t)]
        pltpu.make_async_copy(rows, rows, sems.at[slot]).wait()

    pl.when(i > 0)(lambda: drain(1 - cur))
    pl.when(i == pl.num_programs(0) - 1)(lambda: drain(cur))


def _moe_dispatch(x, slots, ends, padded, n_rows, *, tile, expert_tile):
    m, d = x.shape
    nt = m // tile
    slots_t = slots.reshape(nt, tile, 2).transpose(0, 2, 1).reshape(nt, 1, 2 * tile)
    grid_spec = pltpu.PrefetchScalarGridSpec(
        num_scalar_prefetch=2, grid=(nt,),
        in_specs=[pl.BlockSpec((1, 1, 2 * tile), lambda i, en, pa: (i, 0, 0), memory_space=pltpu.SMEM),
                  pl.BlockSpec((tile, d), lambda i, en, pa: (i, 0))],
        out_specs=pl.BlockSpec(memory_space=pl.ANY),
        scratch_shapes=[pltpu.VMEM((expert_tile, d), x.dtype), pltpu.VMEM((2, tile, d), x.dtype),
                        pltpu.SemaphoreType.DMA, pltpu.SemaphoreType.DMA((2,))])
    kern = functools.partial(_moe_dispatch_kernel, expert_tile=expert_tile)
    return pl.pallas_call(kern, out_shape=jax.ShapeDtypeStruct((n_rows, d), x.dtype), grid_spec=grid_spec,
                          compiler_params=_params("arbitrary"), name="moe_dispatch")(ends, padded, slots_t, x)


def _moe_ffn_kernel(te_ref, nu_ref, xs_ref, win_ref, wdn_ref, y_ref, win_bf, wdn_bf):
    i = pl.program_id(0)

    @pl.when(i < nu_ref[0])
    def _():
        prev = te_ref[jnp.maximum(i - 1, 0)]

        @pl.when((i == 0) | (te_ref[i] != prev))
        def _():
            win_bf[...] = win_ref[0].astype(BF16)
            wdn_bf[...] = wdn_ref[0].astype(BF16)

        h = _dot(xs_ref[...], win_bf[...], False)
        hg, hu = h[:, :D_EXPERT], h[:, D_EXPERT:]
        y_ref[...] = _dot(hg * jax.nn.sigmoid(hg) * hu, wdn_bf[...], False)

    @pl.when(i >= nu_ref[0])
    def _():
        y_ref[...] = jnp.zeros(y_ref.shape, F32)


def _moe_ffn(xs, tile_expert, n_used, w_in, w_down, *, tile):
    n_rows, d = xs.shape
    n_tiles = n_rows // tile
    grid_spec = pltpu.PrefetchScalarGridSpec(
        num_scalar_prefetch=2, grid=(n_tiles,),
        in_specs=[pl.BlockSpec((tile, d), lambda i, te, nu: (jnp.minimum(i, nu[0] - 1), 0)),
                  pl.BlockSpec((1, d, 2 * D_EXPERT), lambda i, te, nu: (te[i], 0, 0)),
                  pl.BlockSpec((1, D_EXPERT, d), lambda i, te, nu: (te[i], 0, 0))],
        out_specs=pl.BlockSpec((tile, d), lambda i, te, nu: (i, 0)),
        scratch_shapes=[pltpu.VMEM((d, 2 * D_EXPERT), BF16), pltpu.VMEM((D_EXPERT, d), BF16)])
    return pl.pallas_call(_moe_ffn_kernel, out_shape=jax.ShapeDtypeStruct((n_rows, d), F32),
                          grid_spec=grid_spec, compiler_params=_params("arbitrary"),
                          name="moe_ffn")(tile_expert, n_used, xs, w_in, w_down)


def _moe_combine_kernel(slots_ref, next_slots_ref, y_ref, g_ref, res_ref, lg_ref, lb_ref, *refs):
    out_refs, (ya_ref, yb_ref, sems) = refs[:-3], refs[-3:]
    i = pl.program_id(0)
    t = res_ref.shape[0]
    cur = lax.rem(i, 2)

    def gather(s_ref, buf):
        def start(r, carry):
            _row_copy(y_ref, s_ref[0, 0, r], ya_ref.at[buf], r, sems.at[buf]).start()
            _row_copy(y_ref, s_ref[0, 0, t + r], yb_ref.at[buf], r, sems.at[buf]).start()
            return carry

        lax.fori_loop(0, t, start, 0, unroll=4)

    pl.when(i == 0)(lambda: gather(slots_ref, 0))
    pl.when(i + 1 < pl.num_programs(0))(lambda: gather(next_slots_ref, 1 - cur))
    for buf_ref in (ya_ref, yb_ref):
        pltpu.make_async_copy(y_ref.at[pl.ds(0, t)], buf_ref.at[cur], sems.at[cur]).wait()
    g = g_ref[...]
    sub = g[:, 0:1] * ya_ref[cur] + g[:, 1:2] * yb_ref[cur]
    y = _layer_norm_rows(DEEPNORM_ALPHA * res_ref[...] + sub, lg_ref[...], lb_ref[...])
    for o in out_refs:
        o[...] = y.astype(o.dtype)


def _moe_combine(y, slots, gates2, res, ln_g, ln_b, *, tile, out_dtypes):
    m, d = res.shape
    nt = m // tile
    slots_t = slots.reshape(nt, tile, 2).transpose(0, 2, 1).reshape(nt, 1, 2 * tile)
    row = pl.BlockSpec((tile, d), lambda i: (i, 0))
    vec = pl.BlockSpec((1, d), lambda i: (0, 0))
    return pl.pallas_call(
        _moe_combine_kernel, out_shape=[jax.ShapeDtypeStruct((m, d), dt) for dt in out_dtypes], grid=(nt,),
        in_specs=[pl.BlockSpec((1, 1, 2 * tile), lambda i: (i, 0, 0), memory_space=pltpu.SMEM),
                  pl.BlockSpec((1, 1, 2 * tile), lambda i: (jnp.minimum(i + 1, nt - 1), 0, 0),
                               memory_space=pltpu.SMEM),
                  pl.BlockSpec(memory_space=pl.ANY),
                  pl.BlockSpec((tile, 2), lambda i: (i, 0)), row, vec, vec],
        out_specs=[row for _ in out_dtypes],
        scratch_shapes=[pltpu.VMEM((2, tile, d), F32), pltpu.VMEM((2, tile, d), F32),
                        pltpu.SemaphoreType.DMA((2,))],
        compiler_params=_params("arbitrary"), name="moe_combine",
    )(slots_t, slots_t, y, gates2, res, ln_g.reshape(1, d), ln_b.reshape(1, d))


def _post_norm_kernel(x_ref, sub_ref, g_ref, b_ref, *out_refs):
    y = _layer_norm_rows(DEEPNORM_ALPHA * x_ref[...] + sub_ref[...], g_ref[...], b_ref[...])
    for o in out_refs:
        o[...] = y.astype(o.dtype)


def _post_norm(x, sub, g, b, *, tm, out_dtypes):
    m, d = x.shape
    tm = min(tm, m)
    row = pl.BlockSpec((tm, d), lambda i: (i, 0))
    vec = pl.BlockSpec((1, d), lambda i: (0, 0))
    return pl.pallas_call(
        _post_norm_kernel, out_shape=[jax.ShapeDtypeStruct((m, d), dt) for dt in out_dtypes],
        grid=(m // tm,), in_specs=[row, row, vec, vec], out_specs=[row for _ in out_dtypes],
        compiler_params=_params("arbitrary"), name="post_norm",
    )(x, sub, g.reshape(1, d), b.reshape(1, d))


PROMPT_TM = 1024
PROMPT_LN_TM = 256
MOE_TM = 512
MOE_ROW_TILE = 256
ATTN_TQ = 512
DSA_TQ = 128
DSA_WIDTH_STEP = 512
DIFF_PAGES_PER_STEP = 8
DSA_PAGES_PER_STEP = 16


def _moe_prompt(xf, xb, router_w, router_bias, w_in, w_down, ln_g, ln_b):
    del xb
    routed = _router(xf, router_w, router_bias, tm=PROMPT_TM, full_f32=False)
    slots, gates2, tile_expert, n_used, n_tiles, ends, padded = _moe_plan(routed, MOE_TM)
    xs = _moe_dispatch(xf, slots, ends, padded, n_tiles * MOE_TM, tile=MOE_ROW_TILE, expert_tile=MOE_TM)
    y = _moe_ffn(xs, tile_expert, n_used, w_in, w_down, tile=MOE_TM)
    return _moe_combine(y, slots, gates2, xf, ln_g, ln_b, tile=MOE_ROW_TILE, out_dtypes=(F32, BF16))


def _moe_sample(xs, router_w, router_bias, w_in, w_down, ln_g, ln_b):
    n = xs.shape[0]
    xpad = jnp.pad(xs, ((0, LANES - n), (0, 0)))
    gates = _router(xpad, router_w, router_bias, tm=LANES, full_f32=True)[:n]
    (y,) = _moe_dense(xs, gates, w_in, w_down, xs, ln_g, ln_b, tm=n, precise=True, out_dtypes=(F32,))
    return y


def _diff_layer(xpf, xpb, xs, cache_k, cache_v, page_table, w_qkv, w_o, lq1, lk1, lq2, lk2, subln_g,
                ln_g, ln_b, layer_idx, b, t):
    lam_init = 0.8 - 0.6 * math.exp(-0.3 * layer_idx)
    lam_vecs = jnp.stack([lq1, lk1, lq2, lk2]).astype(F32)
    dqk = 2 * A_HEADS * A_HEAD_DIM
    cos_p, sin_p = _rope_tables(jnp.arange(t, dtype=jnp.int32), A_HEAD_DIM)
    wb = w_qkv.astype(BF16)
    tm = PROMPT_TM
    (qb,) = _proj(xpb, wb, col_start=0, n_cols=dqk, tm=tm, tn=1024, rope=(cos_p, sin_p, A_HEAD_DIM),
                  out_dtypes=(BF16,), out_scale=A_HEAD_DIM ** -0.5 * LOG2_E, name="diff_q")
    kf, kb = _proj(xpb, wb, col_start=dqk, n_cols=dqk, tm=tm, tn=1024, rope=(cos_p, sin_p, A_HEAD_DIM),
                   out_dtypes=(F32, BF16), name="diff_k")
    vf, vb = _proj(xpb, wb, col_start=2 * dqk, n_cols=dqk, tm=tm, tn=1024, out_dtypes=(F32, BF16),
                   name="diff_v")
    o = _diff_attn_prompt(qb.reshape(b, t, dqk), kb.reshape(b, t, dqk), vb.reshape(b, t, dqk),
                          lam_vecs, subln_g, lam_init, tq=ATTN_TQ)
    x1f, x1b = _proj(o.reshape(b * t, dqk), w_o.astype(BF16), tm=PROMPT_LN_TM, tn=D_MODEL,
                     ln=(xpf, ln_g, ln_b), out_dtypes=(F32, BF16), name="diff_wo")
    bs = xs.shape[0]
    pos_s = jnp.full((bs,), PAST_LEN, jnp.int32)
    cos_s, sin_s = _rope_tables(pos_s, A_HEAD_DIM)
    (qs,) = _proj(xs, w_qkv, col_start=0, n_cols=dqk, tm=bs, tn=1024, precise=True,
                  rope=(cos_s, sin_s, A_HEAD_DIM), name="diff_q_s")
    (ks,) = _proj(xs, w_qkv, col_start=dqk, n_cols=dqk, tm=bs, tn=1024, precise=True,
                  rope=(cos_s, sin_s, A_HEAD_DIM), name="diff_k_s")
    (vs,) = _proj(xs, w_qkv, col_start=2 * dqk, n_cols=dqk, tm=bs, tn=1024, precise=True, name="diff_v_s")
    n_pool = cache_k.shape[0]

    def streams_first(a):
        a = a.reshape(bs, A_HEADS, 2, A_HEAD_DIM)
        return a.transpose(0, 2, 1, 3).reshape(bs, 2 * A_HEADS, A_HEAD_DIM)

    vs_rows = vs.reshape(bs, A_HEADS, 2 * A_HEAD_DIM)
    os_ = _paged_attn(page_table, streams_first(qs),
                      cache_k.reshape(n_pool, PAGE_SIZE * 2 * A_HEADS, A_HEAD_DIM),
                      cache_v.reshape(n_pool, PAGE_SIZE * A_HEADS, 2 * A_HEAD_DIM),
                      streams_first(ks), jnp.concatenate([vs_rows, vs_rows], axis=1),
                      mode="diff", pages_per_step=DIFF_PAGES_PER_STEP, lam_vecs=lam_vecs, subln_g=subln_g,
                      lam_init=lam_init)
    (xs1,) = _proj(os_.reshape(bs, dqk), w_o, tm=bs, tn=D_MODEL, precise=True, ln=(xs, ln_g, ln_b),
                   name="diff_wo_s")
    state = (kf.reshape(b, t, 2 * A_HEADS, A_HEAD_DIM), vf.reshape(b, t, A_HEADS, 2 * A_HEAD_DIM),
             ks.reshape(bs, 1, 2 * A_HEADS, A_HEAD_DIM), vs.reshape(bs, 1, A_HEADS, 2 * A_HEAD_DIM))
    return x1f, x1b, xs1, state


def _dsa_layer(xpf, xpb, xs, cache_k, cache_v, cache_kidx, page_table, w_in, w_o, ln_g, ln_b, b, t):
    dq = B_HEADS * B_HEAD_DIM
    dkv = B_KV_HEADS * B_HEAD_DIM
    dqi = IDX_HEADS * IDX_DIM
    c_k, c_v, c_qi, c_tail = dq, dq + dkv, dq + 2 * dkv, dq + 2 * dkv + dqi
    w_tail = jnp.pad(w_in[:, c_tail:], ((0, 0), (0, LANES - (IDX_DIM + IDX_HEADS))))

    def project(x, w, wt, tm, precise, cos_h, sin_h, cos_i, sin_i, tag):
        both = (F32,) if precise else (F32, BF16)
        low = (F32,) if precise else (BF16,)
        kw = dict(tm=tm, precise=precise)
        q = _proj(x, w, col_start=0, n_cols=dq, tn=1024, rope=(cos_h, sin_h, B_HEAD_DIM), out_dtypes=low,
                  out_scale=1.0 if precise else B_HEAD_DIM ** -0.5 * LOG2_E, name="dsa_q" + tag, **kw)
        k = _proj(x, w, col_start=c_k, n_cols=dkv, tn=dkv, rope=(cos_h, sin_h, B_HEAD_DIM),
                  out_dtypes=both, name="dsa_k" + tag, **kw)
        v = _proj(x, w, col_start=c_v, n_cols=dkv, tn=dkv, out_dtypes=both, name="dsa_v" + tag, **kw)
        qi = _proj(x, w, col_start=c_qi, n_cols=dqi, tn=dqi, rope=(cos_i, sin_i, IDX_DIM),
                   out_dtypes=low, name="dsa_qi" + tag, **kw)
        ki = _proj(x, wt, tn=LANES, rope=(cos_i, sin_i, IDX_DIM), out_dtypes=both,
                   name="dsa_ki" + tag, **kw)
        (wi,) = _proj(x, wt, tn=LANES, name="dsa_wi" + tag, **kw)
        return q, k, v, qi, ki, wi[:, IDX_DIM:IDX_DIM + IDX_HEADS]

    pos_p = jnp.arange(t, dtype=jnp.int32)
    q, k, v, qi, ki, wi = project(xpb, w_in.astype(BF16), w_tail.astype(BF16), PROMPT_TM, False,
                                  *_rope_tables(pos_p, B_HEAD_DIM), *_rope_tables(pos_p, IDX_DIM), "")
    o = _dsa_prompt(qi[0].reshape(b, t, dqi), ki[1][:, :IDX_DIM].reshape(b, t, IDX_DIM),
                    wi.reshape(b, t, IDX_HEADS), q[0].reshape(b, t, dq), k[1].reshape(b, t, dkv),
                    v[1].reshape(b, t, dkv), tq=DSA_TQ)
    x1f, x1b = _proj(o.reshape(b * t, dq), w_o.astype(BF16), tm=PROMPT_LN_TM, tn=D_MODEL,
                     ln=(xpf, ln_g, ln_b), out_dtypes=(F32, BF16), name="dsa_wo")
    bs = xs.shape[0]
    pos_s = jnp.full((bs,), PAST_LEN, jnp.int32)
    qs, ks, vs, qis, kis, wis = project(xs, w_in, w_tail, bs, True, *_rope_tables(pos_s, B_HEAD_DIM),
                                        *_rope_tables(pos_s, IDX_DIM), "_s")
    qs, ks, vs, qis, kis = qs[0], ks[0], vs[0], qis[0], kis[0][:, :IDX_DIM]
    qis3 = qis.reshape(bs, IDX_HEADS, IDX_DIM)
    wis3 = wis.reshape(bs, IDX_HEADS, 1)
    n_pages = page_table.shape[1]
    past_scores = _idx_scores_paged(page_table, qis3, wis3, jnp.swapaxes(cache_kidx, 1, 2),
                                    pages_per_step=DSA_PAGES_PER_STEP)
    new_page = jnp.pad(kis.reshape(bs, IDX_DIM, 1), ((0, 0), (0, 0), (0, PAGE_SIZE - 1)))
    new_scores = _idx_scores_paged(jnp.arange(bs, dtype=jnp.int32).reshape(bs, 1), qis3, wis3, new_page,
                                   pages_per_step=1)
    n_past = n_pages * PAGE_SIZE
    all_scores = jnp.concatenate(
        [past_scores.reshape(bs, n_past), new_scores[:, 0, :1], jnp.full((bs, LANES - 1), -jnp.inf, F32)], axis=1)
    bias_all = _topk_bias(all_scores, min(IDX_TOPK_MAX, (n_past + 1) // 4))
    n_pool = cache_k.shape[0]
    per_head = lambda a: jnp.repeat(a.reshape(bs, B_KV_HEADS, B_HEAD_DIM), B_GROUP, axis=1)
    os_ = _paged_attn(page_table, qs.reshape(bs, B_HEADS, B_HEAD_DIM),
                      cache_k.reshape(n_pool, PAGE_SIZE * B_KV_HEADS, B_HEAD_DIM),
                      cache_v.reshape(n_pool, PAGE_SIZE * B_KV_HEADS, B_HEAD_DIM),
                      per_head(ks), per_head(vs), mode="dsa", pages_per_step=DSA_PAGES_PER_STEP,
                      bias=jnp.repeat(bias_all[:, :n_past], B_KV_HEADS, axis=1).reshape(
                          bs, n_pages, PAGE_SIZE * B_KV_HEADS),
                      bias_new=bias_all[:, n_past:n_past + 1].reshape(bs, 1, 1))
    (xs1,) = _proj(os_.reshape(bs, dq), w_o, tm=bs, tn=D_MODEL, precise=True, ln=(xs, ln_g, ln_b),
                   name="dsa_wo_s")
    state = (k[0].reshape(b, t, B_KV_HEADS, B_HEAD_DIM), v[0].reshape(b, t, B_KV_HEADS, B_HEAD_DIM),
             ki[0][:, :IDX_DIM].reshape(b, t, IDX_DIM),
             ks.reshape(bs, 1, B_KV_HEADS, B_HEAD_DIM), vs.reshape(bs, 1, B_KV_HEADS, B_HEAD_DIM),
             kis.reshape(bs, 1, IDX_DIM))
    return x1f, x1b, xs1, state


def _pool_layer(xpf, xs, state_pool, w_group, scale, ln_g, ln_b, b, t):
    xp3 = xpf.reshape(b, t, D_MODEL)
    sub = _pool_prompt(xp3, w_group.astype(BF16), scale)
    x1f, x1b = _post_norm(xpf, sub.reshape(b * t, D_MODEL), ln_g, ln_b, tm=PROMPT_LN_TM,
                          out_dtypes=(F32, BF16))
    bs = xs.shape[0]
    xs_ext = jnp.concatenate([state_pool.astype(xs.dtype), xs.reshape(bs, 1, D_MODEL)], axis=1)
    sub_s = _pool_sample(xs_ext, w_group, scale)
    (xs1,) = _post_norm(xs, sub_s, ln_g, ln_b, tm=bs, out_dtypes=(F32,))
    return x1f, x1b, xs1, (xp3[:, -POOL_STATE_LEN:], xs_ext[:, -POOL_STATE_LEN:])


def kernel(x_prompt, x_sample, cache_l0_k, cache_l0_v, cache_l1_k, cache_l1_v, cache_l1_kidx, state_l2_pool, cache_l3_k, cache_l3_v, page_table, router_w, router_bias, l0_w_qkv, l0_w_o, l0_lam_q1, l0_lam_k1, l0_lam_q2, l0_lam_k2, l0_subln_g, l0_ln1_g, l0_ln1_b, l0_moe_w_in, l0_moe_w_down, l0_ln2_g, l0_ln2_b, l1_w_in, l1_w_o, l1_ln1_g, l1_ln1_b, l1_moe_w_in, l1_moe_w_down, l1_ln2_g, l1_ln2_b, l2_w_group, l2_scale, l2_ln1_g, l2_ln1_b, l2_moe_w_in, l2_moe_w_down, l2_ln2_g, l2_ln2_b, l3_w_qkv, l3_w_o, l3_lam_q1, l3_lam_k1, l3_lam_q2, l3_lam_k2, l3_subln_g, l3_ln1_g, l3_ln1_b, l3_moe_w_in, l3_moe_w_down, l3_ln2_g, l3_ln2_b):
    b, t, d = x_prompt.shape
    bs = x_sample.shape[0]
    xpf = x_prompt.reshape(b * t, d)
    xpb = xpf.astype(BF16)
    xs = x_sample.reshape(bs, d)
    moe_p = [(l0_moe_w_in, l0_moe_w_down, l0_ln2_g, l0_ln2_b), (l1_moe_w_in, l1_moe_w_down, l1_ln2_g, l1_ln2_b),
             (l2_moe_w_in, l2_moe_w_down, l2_ln2_g, l2_ln2_b), (l3_moe_w_in, l3_moe_w_down, l3_ln2_g, l3_ln2_b)]
    states = []
    for i in range(DEPTH):
        if i == 0:
            xpf, xpb, xs, st = _diff_layer(xpf, xpb, xs, cache_l0_k, cache_l0_v, page_table, l0_w_qkv, l0_w_o,
                                           l0_lam_q1, l0_lam_k1, l0_lam_q2, l0_lam_k2, l0_subln_g,
                                           l0_ln1_g, l0_ln1_b, i, b, t)
        elif i == 1:
            xpf, xpb, xs, st = _dsa_layer(xpf, xpb, xs, cache_l1_k, cache_l1_v, cache_l1_kidx, page_table,
                                          l1_w_in, l1_w_o, l1_ln1_g, l1_ln1_b, b, t)
        elif i == 2:
            xpf, xpb, xs, st = _pool_layer(xpf, xs, state_l2_pool, l2_w_group, l2_scale, l2_ln1_g, l2_ln1_b, b, t)
        else:
            xpf, xpb, xs, st = _diff_layer(xpf, xpb, xs, cache_l3_k, cache_l3_v, page_table, l3_w_qkv, l3_w_o,
                                           l3_lam_q1, l3_lam_k1, l3_lam_q2, l3_lam_k2, l3_subln_g,
                                           l3_ln1_g, l3_ln1_b, i, b, t)
        states.append(st)
        w_in, w_down, ln_g, ln_b = moe_p[i]
        xpf, xpb = _moe_prompt(xpf, xpb, router_w, router_bias, w_in, w_down, ln_g, ln_b)
        xs = _moe_sample(xs, router_w, router_bias, w_in, w_down, ln_g, ln_b)
    l0, l1, l2, l3 = states
    return (xpf.reshape(b, t, d), xs.reshape(bs, 1, d),
            l0[0], l0[1], l0[2], l0[3],
            l1[0], l1[1], l1[2], l1[3], l1[4], l1[5],
            l2[0], l2[1],
            l3[0], l3[1], l3[2], l3[3])
```

```python
import functools
import math

import jax
import jax.numpy as jnp
from jax import lax
from jax.experimental import pallas as pl
from jax.experimental.pallas import tpu as pltpu

F32 = jnp.float32
BF16 = jnp.bfloat16

D_MODEL = 2048
DEPTH = 4
PAST_LEN = 16384
PAGE_SIZE = 128
ROPE_THETA = 10000.0
LN_EPS = 1e-5
A_HEADS = 8
A_HEAD_DIM = D_MODEL // A_HEADS // 2
B_HEADS = 16
B_HEAD_DIM = D_MODEL // B_HEADS
B_KV_HEADS = 4
B_GROUP = B_HEADS // B_KV_HEADS
IDX_HEADS = 16
IDX_DIM = 64
IDX_TOPK_MAX = 256
POOL_WINDOWS = (2, 4, 8, 16)
POOL_GROUPS = 4
POOL_GROUP_DIM = D_MODEL // POOL_GROUPS
POOL_STATE_LEN = max(POOL_WINDOWS) - 1
N_EXPERTS = 16
N_EXPERT_GROUPS = 4
EXPERTS_PER_GROUP = N_EXPERTS // N_EXPERT_GROUPS
D_EXPERT = D_MODEL // 4
DEEPNORM_ALPHA = (2 * DEPTH) ** 0.25

LANES = 128
SUBLANES = 8
VMEM_LIMIT_BYTES = 56 * 1024 * 1024

NEG_BIG = -1e30
HIGHEST = lax.Precision.HIGHEST


def _params(*sem):
    return pltpu.CompilerParams(dimension_semantics=sem, vmem_limit_bytes=VMEM_LIMIT_BYTES)


def _split_bf16(a):
    hi = a.astype(BF16)
    return hi, (a - hi.astype(F32)).astype(BF16)


def _dot(a, b, precise):
    if precise:
        a = a.astype(F32)
        m = a.shape[0]
        a_hi_b = a.astype(BF16)
        a_hi = a_hi_b.astype(F32)
        a_both = jnp.concatenate([a_hi, a - a_hi], axis=0).astype(BF16)
        b_hi, b_lo = _split_bf16(b.astype(F32))
        r = jnp.dot(a_both, b_hi, preferred_element_type=F32)
        return r[:m] + r[m:] + jnp.dot(a_hi_b, b_lo, preferred_element_type=F32)
    return jnp.dot(a.astype(BF16), b.astype(BF16), preferred_element_type=F32)


def _dot_nt(a, b, precise):
    dims = (((1,), (1,)), ((), ()))
    if precise:
        return lax.dot_general(a.astype(F32), b.astype(F32), dims, precision=HIGHEST,
                               preferred_element_type=F32)
    return lax.dot_general(a.astype(BF16), b.astype(BF16), dims, preferred_element_type=F32)


def _layer_norm_rows(r, g, b):
    mu = jnp.mean(r, axis=-1, keepdims=True)
    c = r - mu
    var = jnp.mean(c * c, axis=-1, keepdims=True)
    return c * lax.rsqrt(var + LN_EPS) * g + b


def _rope_tables(pos, head_dim):
    half = head_dim // 2
    inv_freq = ROPE_THETA ** (-jnp.arange(half, dtype=F32) * 2.0 / head_dim)
    ang = pos.astype(F32)[:, None] * inv_freq[None, :]
    cos, sin = jnp.cos(ang), jnp.sin(ang)
    reps = LANES // head_dim
    cos_t = jnp.tile(jnp.concatenate([cos, cos], axis=-1), (1, reps))
    sin_t = jnp.tile(jnp.concatenate([-sin, sin], axis=-1), (1, reps))
    return cos_t, sin_t


def _rope_lanes(x, cos_t, sin_t, head_dim):
    half = head_dim // 2
    if head_dim == LANES:
        partner = pltpu.roll(x, half, axis=1)
    else:
        lane = lax.broadcasted_iota(jnp.int32, x.shape, 1)
        first = (lane % head_dim) < half
        partner = jnp.where(first, pltpu.roll(x, LANES - half, axis=1), pltpu.roll(x, half, axis=1))
    return x * cos_t + partner * sin_t


def _proj_kernel(*refs, precise, rope_dim, has_ln, n_out, out_scale):
    it = iter(refs)
    x_ref, w_ref = next(it), next(it)
    cos_ref = sin_ref = res_ref = g_ref = b_ref = None
    if rope_dim:
        cos_ref, sin_ref = next(it), next(it)
    if has_ln:
        res_ref, g_ref, b_ref = next(it), next(it), next(it)
    outs = [next(it) for _ in range(n_out)]
    acc = _dot(x_ref[...], w_ref[...], precise)
    if rope_dim:
        cos_t, sin_t = cos_ref[...], sin_ref[...]
        pieces = [_rope_lanes(acc[:, c * LANES:(c + 1) * LANES], cos_t, sin_t, rope_dim)
                  for c in range(acc.shape[1] // LANES)]
        acc = jnp.concatenate(pieces, axis=1) if len(pieces) > 1 else pieces[0]
    if has_ln:
        acc = _layer_norm_rows(DEEPNORM_ALPHA * res_ref[...] + acc, g_ref[...], b_ref[...])
    if out_scale != 1.0:
        acc = acc * out_scale
    for o in outs:
        o[...] = acc.astype(o.dtype)


def _proj(x, w, *, col_start=0, n_cols=None, tm, tn, precise=False, rope=None, ln=None,
          out_dtypes=(F32,), out_scale=1.0, name="proj"):
    m, k = x.shape
    tm = min(tm, m)
    n_cols = w.shape[1] - col_start if n_cols is None else n_cols
    assert m % tm == 0 and n_cols % tn == 0 and col_start % tn == 0
    cb = col_start // tn
    grid = (n_cols // tn, m // tm)
    in_specs = [pl.BlockSpec((tm, k), lambda j, i: (i, 0)),
                pl.BlockSpec((k, tn), lambda j, i: (0, cb + j))]
    args = [x, w]
    rope_dim = 0
    if rope is not None:
        cos_t, sin_t, rope_dim = rope
        period = cos_t.shape[0] // tm
        in_specs += [pl.BlockSpec((tm, LANES), lambda j, i: (i % period, 0))] * 2
        args += [cos_t, sin_t]
    if ln is not None:
        assert tn == n_cols
        res, g, b = ln
        in_specs += [pl.BlockSpec((tm, tn), lambda j, i: (i, 0)),
                     pl.BlockSpec((1, tn), lambda j, i: (0, 0)),
                     pl.BlockSpec((1, tn), lambda j, i: (0, 0))]
        args += [res, g.reshape(1, -1), b.reshape(1, -1)]
    out_shape = [jax.ShapeDtypeStruct((m, n_cols), dt) for dt in out_dtypes]
    out_specs = [pl.BlockSpec((tm, tn), lambda j, i: (i, j)) for _ in out_dtypes]
    kern = functools.partial(_proj_kernel, precise=precise, rope_dim=rope_dim, has_ln=ln is not None,
                             n_out=len(out_dtypes), out_scale=out_scale)
    res = pl.pallas_call(kern, out_shape=out_shape, grid=grid, in_specs=in_specs, out_specs=out_specs,
                         compiler_params=_params("arbitrary", "arbitrary"), name=name)(*args)
    return res


def _lam_scalar(lam_ref, lam_init):
    v = lam_ref[...]
    a = jnp.sum(v[0:1] * v[1:2], axis=1, keepdims=True)
    b = jnp.sum(v[2:3] * v[3:4], axis=1, keepdims=True)
    return jnp.exp(a) - jnp.exp(b) + lam_init


def _subln(o, g, lam_init):
    ms = jnp.mean(o * o, axis=-1, keepdims=True)
    return o * lax.rsqrt(ms + LN_EPS) * g * (1.0 - lam_init)


def _diff_attn_kernel(lam_ref, g_ref, q_ref, k_ref, v_ref, o_ref, *, tq, lam_init, n_blocks):
    i = pl.program_id(2)
    hd = A_HEAD_DIM
    lam = _lam_scalar(lam_ref, lam_init)
    row = lax.broadcasted_iota(jnp.int32, (tq, tq), 0)
    col = lax.broadcasted_iota(jnp.int32, (tq, tq), 1)
    on_or_below_diag = col <= row

    def branch(c):
        n_left = c * tq
        q = q_ref[0]
        streams = []
        for s_ in range(2):
            lanes = slice(s_ * hd, (s_ + 1) * hd)
            qs = q[:, lanes]
            sd = _dot_nt(qs, k_ref[0, n_left:n_left + tq, lanes], False)
            sd = jnp.where(on_or_below_diag, sd, NEG_BIG)
            m = jnp.max(sd, axis=1, keepdims=True)
            if c > 0:
                sl = _dot_nt(qs, k_ref[0, 0:n_left, lanes], False)
                m = jnp.maximum(m, jnp.max(sl, axis=1, keepdims=True))
            pd = jnp.exp2(sd - m)
            l = jnp.sum(pd, axis=1, keepdims=True)
            o = _dot(pd, v_ref[0, n_left:n_left + tq, :], False)
            if c > 0:
                p_left = jnp.exp2(sl - m)
                l = l + jnp.sum(p_left, axis=1, keepdims=True)
                o = o + _dot(p_left, v_ref[0, 0:n_left, :], False)
            streams.append(o / l)
        o = streams[0] - lam * streams[1]
        o_ref[0] = _subln(o, g_ref[...], lam_init).astype(o_ref.dtype)

    for c in range(n_blocks):
        pl.when(i == c)(functools.partial(branch, c))


LOG2_E = 1.4426950408889634


def _diff_attn_prompt(q, k, v, lam_vecs, subln_g, lam_init, *, tq):
    b, t, _ = q.shape
    w = 2 * A_HEAD_DIM
    grid = (b, A_HEADS, t // tq)
    kern = functools.partial(_diff_attn_kernel, tq=tq, lam_init=lam_init, n_blocks=t // tq)
    return pl.pallas_call(
        kern, out_shape=jax.ShapeDtypeStruct((b, t, A_HEADS * w), BF16), grid=grid,
        in_specs=[pl.BlockSpec((4, A_HEAD_DIM), lambda bi, h, i: (0, 0)),
                  pl.BlockSpec((1, w), lambda bi, h, i: (0, 0)),
                  pl.BlockSpec((1, tq, w), lambda bi, h, i: (bi, i, h)),
                  pl.BlockSpec((1, t, w), lambda bi, h, i: (bi, 0, h)),
                  pl.BlockSpec((1, t, w), lambda bi, h, i: (bi, 0, h))],
        out_specs=pl.BlockSpec((1, tq, w), lambda bi, h, i: (bi, i, h)),
        compiler_params=_params("arbitrary", "arbitrary", "arbitrary"), name="diff_attn_prompt",
    )(lam_vecs, subln_g.reshape(1, w), q, k, v)


def _paged_attn_kernel(pt_ref, *refs, pages_per_step, scale, mode, lam_init):
    g_pages = pages_per_step
    it = iter(refs)
    q_ref = next(it)
    bias_ref = bias_new_ref = lam_ref = g_ref = None
    if mode == "dsa":
        bias_ref, bias_new_ref = next(it), next(it)
    else:
        lam_ref, g_ref = next(it), next(it)
    k_refs = [next(it) for _ in range(g_pages)]
    v_refs = [next(it) for _ in range(g_pages)]
    knew_ref, vnew_ref = next(it), next(it)
    o_ref = next(it)
    m_ref, l_ref, acc_ref = next(it), next(it), next(it)
    j = pl.program_id(1)
    n_steps = pl.num_programs(1)

    @pl.when(j == 0)
    def _():
        m_ref[...] = jnp.full(m_ref.shape, NEG_BIG, F32)
        l_ref[...] = jnp.zeros(l_ref.shape, F32)
        acc_ref[...] = jnp.zeros(acc_ref.shape, F32)

    q = q_ref[0]
    n_rows = q.shape[0]
    if mode == "diff":
        lanes = PAGE_SIZE * A_HEADS
        wanted = lambda r, lane: (lane % A_HEADS) == (r % A_HEADS)
    else:
        lanes = PAGE_SIZE * B_KV_HEADS
        wanted = lambda r, lane: (lane % B_KV_HEADS) == (r // B_GROUP)
    ri = lax.broadcasted_iota(jnp.int32, (n_rows, lanes), 0)
    li = lax.broadcasted_iota(jnp.int32, (n_rows, lanes), 1)
    own = wanted(ri, li)

    def update(scores, values):
        m_old = m_ref[...]
        m_new = m_old
        for s in scores:
            m_new = jnp.maximum(m_new, jnp.max(s, axis=1, keepdims=True))
        alpha = jnp.exp(m_old - m_new)
        l = alpha * l_ref[...]
        acc = alpha * acc_ref[...]
        for s, v_rows in zip(scores, values):
            p = jnp.exp(s - m_new)
            l = l + jnp.sum(p, axis=1, keepdims=True)
            acc = acc + (p * v_rows if s.shape[1] == 1 else _dot(p, v_rows, False))
        l_ref[...] = l
        acc_ref[...] = acc
        m_ref[...] = m_new

    scores = []
    for g in range(g_pages):
        if mode == "diff":
            half = n_rows // 2
            s = jnp.concatenate(
                [_dot_nt(q[c * half:(c + 1) * half], k_refs[g][0, pl.ds(c, lanes, stride=2), :], False)
                 for c in range(2)], axis=0)
        else:
            s = _dot_nt(q, k_refs[g][0], False)
        s = jnp.where(own, s * scale, NEG_BIG)
        if mode == "dsa":
            s = s + bias_ref[0, pl.ds(j * g_pages + g, 1), :]
        scores.append(s)
    update(scores, [v_refs[g][0] for g in range(g_pages)])

    @pl.when(j == n_steps - 1)
    def _():
        s_new = jnp.sum(q * knew_ref[0], axis=1, keepdims=True) * scale
        if mode == "dsa":
            s_new = s_new + bias_new_ref[0]
        update([s_new], [vnew_ref[0]])
        o = acc_ref[...] / l_ref[...]
        if mode == "diff":
            half = n_rows // 2
            o = _subln(o[:half] - _lam_scalar(lam_ref, lam_init) * o[half:], g_ref[...], lam_init)
        o_ref[0] = o


def _paged_attn(page_table, q, cache_k, cache_v, k_new, v_new, *, mode, pages_per_step,
                bias=None, bias_new=None, lam_vecs=None, subln_g=None, lam_init=0.0):
    bs, n_rows, d = q.shape
    n_pages = page_table.shape[1]
    g = pages_per_step
    assert n_pages % g == 0
    grid = (bs, n_pages // g)
    e = cache_v.shape[2]
    in_specs = [pl.BlockSpec((1, n_rows, d), lambda b, j, pt: (b, 0, 0))]
    args = [q]
    if mode == "dsa":
        in_specs += [pl.BlockSpec((1, n_pages, bias.shape[2]), lambda b, j, pt: (b, 0, 0)),
                     pl.BlockSpec((1, 1, 1), lambda b, j, pt: (b, 0, 0))]
        args += [bias, bias_new]
        out_rows = n_rows
    else:
        in_specs += [pl.BlockSpec((4, A_HEAD_DIM), lambda b, j, pt: (0, 0)),
                     pl.BlockSpec((1, e), lambda b, j, pt: (0, 0))]
        args += [lam_vecs, subln_g.reshape(1, -1)]
        out_rows = n_rows // 2

    def page_spec(arr, gi):
        return pl.BlockSpec((1,) + arr.shape[1:], lambda b, j, pt: (pt[b, j * g + gi], 0, 0))

    in_specs += [page_spec(cache_k, gi) for gi in range(g)] + [page_spec(cache_v, gi) for gi in range(g)]
    args += [cache_k] * g + [cache_v] * g
    in_specs += [pl.BlockSpec((1, n_rows, d), lambda b, j, pt: (b, 0, 0)),
                 pl.BlockSpec((1, n_rows, e), lambda b, j, pt: (b, 0, 0))]
    args += [k_new, v_new]
    kern = functools.partial(_paged_attn_kernel, pages_per_step=g, scale=d ** -0.5, mode=mode,
                             lam_init=lam_init)
    grid_spec = pltpu.PrefetchScalarGridSpec(
        num_scalar_prefetch=1, grid=grid, in_specs=in_specs,
        out_specs=pl.BlockSpec((1, out_rows, e), lambda b, j, pt: (b, 0, 0)),
        scratch_shapes=[pltpu.VMEM((n_rows, 1), F32), pltpu.VMEM((n_rows, 1), F32),
                        pltpu.VMEM((n_rows, e), F32)])
    return pl.pallas_call(kern, out_shape=jax.ShapeDtypeStruct((bs, out_rows, e), F32),
                          grid_spec=grid_spec, compiler_params=_params("arbitrary", "arbitrary"),
                          name="paged_attn_" + mode)(page_table, *args)


INT32_MIN = -2 ** 31
KEY_OF_NEG_INF = (-8388608) ^ 0x7FFFFFFF


def _topk_bias_into(bias_ref, scores, k_sel):
    r, n = scores.shape
    scores = jnp.where(scores == 0.0, 0.0, scores)
    bits = pltpu.bitcast(scores, jnp.int32)
    key = jnp.where(bits < 0, bits ^ 0x7FFFFFFF, bits)
    kf = float(k_sel)

    def step(t, ans):
        cand = ans + jnp.left_shift(jnp.int32(1), 31 - t)
        cnt = jnp.sum(jnp.where(key >= cand, 1.0, 0.0), axis=1, keepdims=True)
        return jnp.where(cnt >= kf, cand, ans)

    thr = lax.fori_loop(0, 32, step, jnp.full((r, 1), INT32_MIN, jnp.int32), unroll=4)
    gt = key > thr
    eq = key == thr
    cnt_gt = jnp.sum(jnp.where(gt, 1.0, 0.0), axis=1, keepdims=True)
    cnt_eq = jnp.sum(jnp.where(eq, 1.0, 0.0), axis=1, keepdims=True)
    bias_ref[...] = jnp.where(key >= thr, 0.0, NEG_BIG)
    tie = jnp.max(jnp.where((cnt_gt + cnt_eq > kf) & (thr > KEY_OF_NEG_INF), 1.0, 0.0))

    @pl.when(tie > 0.0)
    def _():
        need = kf - cnt_gt
        ri = lax.broadcasted_iota(jnp.int32, (LANES, LANES), 0)
        ci = lax.broadcasted_iota(jnp.int32, (LANES, LANES), 1)
        tri = jnp.where(ri < ci, 1.0, 0.0).astype(BF16)
        run = jnp.zeros((r, 1), F32)
        for c in range(n // LANES):
            sl = slice(c * LANES, (c + 1) * LANES)
            eq_c = jnp.where(eq[:, sl], 1.0, 0.0)
            before = jnp.dot(eq_c.astype(BF16), tri, preferred_element_type=F32) + run
            take = gt[:, sl] | (eq[:, sl] & (before < need))
            bias_ref[:, sl] = jnp.where(take, 0.0, NEG_BIG)
            run = run + jnp.sum(eq_c, axis=1, keepdims=True)


def _dsa_prompt_kernel(qi_ref, kibd_ref, wi_ref, q_ref, k_ref, v_ref, o_ref, bias_ref, *, tq, k_sel,
                       width_step):
    i = pl.program_id(1)
    t = k_ref.shape[1]

    def branch(n):
        wi = wi_ref[0]
        qi = qi_ref[0]
        score = jnp.zeros((tq, n), F32)
        for hp in range(IDX_HEADS // 2):
            q_pair = qi[:, hp * LANES:(hp + 1) * LANES]
            for c in range(2):
                d = _dot_nt(q_pair, kibd_ref[0, c * t:c * t + n, :], False)
                score = score + wi[:, 2 * hp + c:2 * hp + c + 1] * jnp.maximum(d, 0.0)
        row = i * tq + lax.broadcasted_iota(jnp.int32, (tq, n), 0)
        col = lax.broadcasted_iota(jnp.int32, (tq, n), 1)
        causal = col <= row
        bias_view = bias_ref.at[:, pl.ds(0, n)]
        _topk_bias_into(bias_view, jnp.where(causal, score, -jnp.inf), k_sel)
        bias = jnp.where(causal, bias_view[...], NEG_BIG)
        bias4 = jnp.concatenate([bias] * B_GROUP, axis=0)
        q = q_ref[0]
        outs = []
        for kv in range(B_KV_HEADS):
            qg = jnp.concatenate([q[:, (kv * B_GROUP + r) * B_HEAD_DIM:(kv * B_GROUP + r + 1) * B_HEAD_DIM]
                                  for r in range(B_GROUP)], axis=0)
            kg = k_ref[0, 0:n, kv * B_HEAD_DIM:(kv + 1) * B_HEAD_DIM]
            vg = v_ref[0, 0:n, kv * B_HEAD_DIM:(kv + 1) * B_HEAD_DIM]
            s = _dot_nt(qg, kg, False) + bias4
            p = jnp.exp2(s - jnp.max(s, axis=1, keepdims=True))
            l = jnp.sum(p, axis=1, keepdims=True)
            og = _dot(p, vg, False) / l
            outs += [og[r * tq:(r + 1) * tq] for r in range(B_GROUP)]
        o_ref[0] = jnp.concatenate(outs, axis=1).astype(o_ref.dtype)

    for cls in range(t // width_step):
        pl.when((i * tq + tq - 1) // width_step == cls)(functools.partial(branch, (cls + 1) * width_step))


def _dsa_prompt(qi, ki, wi, q, k, v, *, tq):
    b, t, _ = q.shape
    k_sel = min(IDX_TOPK_MAX, t // 4)
    zeros = jnp.zeros_like(ki)
    kibd = jnp.concatenate([jnp.concatenate([ki, zeros], axis=-1),
                            jnp.concatenate([zeros, ki], axis=-1)], axis=1)
    kvw = B_KV_HEADS * B_HEAD_DIM
    kern = functools.partial(_dsa_prompt_kernel, tq=tq, k_sel=k_sel, width_step=min(DSA_WIDTH_STEP, t))
    return pl.pallas_call(
        kern, out_shape=jax.ShapeDtypeStruct((b, t, D_MODEL), BF16), grid=(b, t // tq),
        in_specs=[pl.BlockSpec((1, tq, IDX_HEADS * IDX_DIM), lambda bi, i: (bi, i, 0)),
                  pl.BlockSpec((1, 2 * t, LANES), lambda bi, i: (bi, 0, 0)),
                  pl.BlockSpec((1, tq, IDX_HEADS), lambda bi, i: (bi, i, 0)),
                  pl.BlockSpec((1, tq, D_MODEL), lambda bi, i: (bi, i, 0)),
                  pl.BlockSpec((1, t, kvw), lambda bi, i: (bi, 0, 0)),
                  pl.BlockSpec((1, t, kvw), lambda bi, i: (bi, 0, 0))],
        out_specs=pl.BlockSpec((1, tq, D_MODEL), lambda bi, i: (bi, i, 0)),
        scratch_shapes=[pltpu.VMEM((tq, t), F32)],
        compiler_params=_params("arbitrary", "arbitrary"), name="dsa_prompt",
    )(qi, kibd, wi, q, k, v)


def _idx_scores_kernel(pt_ref, qi_ref, w_ref, *refs, pages_per_step):
    page_refs, o_ref = refs[:pages_per_step], refs[pages_per_step]
    qi = qi_ref[0]
    w = w_ref[0]
    rows = []
    for g in range(pages_per_step):
        d = _dot(qi, page_refs[g][0], True)
        rows.append(jnp.sum(w * jnp.maximum(d, 0.0), axis=0, keepdims=True))
    o_ref[0] = jnp.concatenate(rows, axis=0)


def _idx_scores_paged(page_table, qi, wi, cache_kidx, *, pages_per_step):
    bs = qi.shape[0]
    n_pages = page_table.shape[1]
    g = pages_per_step

    def page_spec(gi):
        return pl.BlockSpec((1, IDX_DIM, PAGE_SIZE), lambda b, j, pt: (pt[b, j * g + gi], 0, 0))

    grid_spec = pltpu.PrefetchScalarGridSpec(
        num_scalar_prefetch=1, grid=(bs, n_pages // g),
        in_specs=[pl.BlockSpec((1, IDX_HEADS, IDX_DIM), lambda b, j, pt: (b, 0, 0)),
                  pl.BlockSpec((1, IDX_HEADS, 1), lambda b, j, pt: (b, 0, 0))]
        + [page_spec(gi) for gi in range(g)],
        out_specs=pl.BlockSpec((1, g, PAGE_SIZE), lambda b, j, pt: (b, j, 0)))
    kern = functools.partial(_idx_scores_kernel, pages_per_step=g)
    return pl.pallas_call(kern, out_shape=jax.ShapeDtypeStruct((bs, n_pages, PAGE_SIZE), F32),
                          grid_spec=grid_spec, compiler_params=_params("arbitrary", "arbitrary"),
                          name="dsa_idx_scores")(page_table, qi, wi, *([cache_kidx] * g))


def _topk_bias_kernel(s_ref, o_ref, *, k_sel):
    _topk_bias_into(o_ref, s_ref[...], k_sel)


def _topk_bias(scores, k_sel):
    return pl.pallas_call(functools.partial(_topk_bias_kernel, k_sel=k_sel),
                          out_shape=jax.ShapeDtypeStruct(scores.shape, F32),
                          compiler_params=pltpu.CompilerParams(vmem_limit_bytes=VMEM_LIMIT_BYTES),
                          name="topk_bias")(scores)


def _pool_prompt_kernel(x_ref, w_ref, sc_ref, o_ref, pad_ref, *, window, rows):
    t = x_ref.shape[1]
    halo = 2 * SUBLANES
    pad_ref[0:halo, :] = jnp.zeros((halo, pad_ref.shape[1]), F32)
    pad_ref[halo:, :] = x_ref[0]
    w = w_ref[0]
    sc = sc_ref[...]

    def body(c, carry):
        r0 = pl.multiple_of(c * rows, rows)
        xh = pad_ref[pl.ds(r0, rows + halo), :]
        x = xh[halo:, :]
        win = x
        for jj in range(1, window):
            win = win + xh[halo - jj:halo - jj + rows, :]
        pos = r0 + lax.broadcasted_iota(jnp.int32, (rows, 1), 0)
        count = jnp.minimum(window, pos + 1).astype(F32)
        mixed = win / count - x
        o_ref[0, pl.ds(r0, rows), :] = _dot(mixed, w, False) * sc
        return carry

    lax.fori_loop(0, t // rows, body, 0)


def _pool_prompt(x, w_group_bf16, scale):
    b, t, d = x.shape
    gd = POOL_GROUP_DIM
    outs = []
    for g, window in enumerate(POOL_WINDOWS):
        kern = functools.partial(_pool_prompt_kernel, window=window, rows=256)
        outs.append(pl.pallas_call(
            kern, out_shape=jax.ShapeDtypeStruct((b, t, gd), F32), grid=(b,),
            in_specs=[pl.BlockSpec((1, t, gd), lambda bi, g=g: (bi, 0, g)),
                      pl.BlockSpec((1, gd, gd), lambda bi, g=g: (g, 0, 0)),
                      pl.BlockSpec((1, gd), lambda bi, g=g: (0, g))],
            out_specs=pl.BlockSpec((1, t, gd), lambda bi: (bi, 0, 0)),
            scratch_shapes=[pltpu.VMEM((t + 2 * SUBLANES, gd), F32)],
            compiler_params=_params("arbitrary"), name="pool_prompt_w%d" % window,
        )(x, w_group_bf16, scale.reshape(1, d)))
    return jnp.concatenate(outs, axis=-1)


def _pool_sample_kernel(x_ref, w_ref, sc_ref, o_ref):
    g = pl.program_id(0)
    window = jnp.left_shift(jnp.int32(2), g)
    x = x_ref[...]
    n = x.shape[1]
    rowi = lax.broadcasted_iota(jnp.int32, x.shape, 1)
    win = jnp.sum(jnp.where(rowi >= n - window, x, 0.0), axis=1)
    mixed = win / window.astype(F32) - x[:, n - 1, :]
    o_ref[...] = _dot(mixed, w_ref[0], True) * sc_ref[...]


def _pool_sample(x_ext, w_group, scale):
    b, n, d = x_ext.shape
    gd = POOL_GROUP_DIM
    return pl.pallas_call(
        _pool_sample_kernel, out_shape=jax.ShapeDtypeStruct((b, d), F32), grid=(POOL_GROUPS,),
        in_specs=[pl.BlockSpec((b, n, gd), lambda g: (0, 0, g)),
                  pl.BlockSpec((1, gd, gd), lambda g: (g, 0, 0)),
                  pl.BlockSpec((1, gd), lambda g: (0, g))],
        out_specs=pl.BlockSpec((b, gd), lambda g: (0, g)),
        compiler_params=_params("arbitrary"), name="pool_sample",
    )(x_ext, w_group, scale.reshape(1, d))


def _router_kernel(x_ref, rw_ref, rb_ref, g_ref, *, full_f32):
    tm = x_ref.shape[0]
    if full_f32:
        logits = jnp.dot(x_ref[...], rw_ref[...], precision=HIGHEST, preferred_element_type=F32)
    else:
        x_hi, x_lo = _split_bf16(x_ref[...])
        w_hi, w_lo = _split_bf16(rw_ref[...])
        logits = (jnp.dot(x_hi, w_hi, preferred_element_type=F32)
                  + jnp.dot(x_lo, w_hi, preferred_element_type=F32)
                  + jnp.dot(x_hi, w_lo, preferred_element_type=F32))
    lane = lax.broadcasted_iota(jnp.int32, logits.shape, 1)
    logits = jnp.where(lane < N_EXPERTS, logits, NEG_BIG)
    e = jnp.exp(logits - jnp.max(logits, axis=1, keepdims=True))
    aff_t = (e / jnp.sum(e, axis=1, keepdims=True)).T
    bias = rb_ref[...]
    aff = [aff_t[x:x + 1, :] for x in range(N_EXPERTS)]
    sel = [aff[x] + bias[x:x + 1, :] for x in range(N_EXPERTS)]
    gsz = EXPERTS_PER_GROUP
    best_score = best_group = None
    for g in range(N_EXPERT_GROUPS):
        a, b, c, d = sel[g * gsz:(g + 1) * gsz]
        hi1, lo1, hi2, lo2 = jnp.maximum(a, b), jnp.minimum(a, b), jnp.maximum(c, d), jnp.minimum(c, d)
        score = jnp.maximum(hi1, hi2) + jnp.maximum(jnp.minimum(hi1, hi2), jnp.maximum(lo1, lo2))
        if g == 0:
            best_score, best_group = score, jnp.zeros(score.shape, jnp.int32)
        else:
            better = score > best_score
            best_group = jnp.where(better, g, best_group)
            best_score = jnp.where(better, score, best_score)
    chosen = []
    for x in range(N_EXPERTS):
        g = x // gsz
        rank = jnp.zeros((1, tm), F32)
        for y in range(g * gsz, (g + 1) * gsz):
            if y == x:
                continue
            ahead = (sel[y] > sel[x]) | ((sel[y] == sel[x]) & (y < x))
            rank = rank + jnp.where(ahead, 1.0, 0.0)
        chosen.append((best_group == g) & (rank < 2.0))
    top_sum = jnp.zeros((1, tm), F32)
    for x in range(N_EXPERTS):
        top_sum = top_sum + jnp.where(chosen[x], aff[x], 0.0)
    rows = [jnp.where(chosen[x], aff[x] / top_sum, 0.0) for x in range(N_EXPERTS)]
    rows += [jnp.where(chosen[x], 1.0, 0.0) for x in range(N_EXPERTS)]
    rows.append(jnp.zeros((LANES - 2 * N_EXPERTS, tm), F32))
    g_ref[...] = jnp.concatenate(rows, axis=0).T


def _router(x, router_w, router_bias, *, tm, full_f32):
    m, d = x.shape
    tm = min(tm, m)
    rw = jnp.pad(router_w, ((0, 0), (0, LANES - N_EXPERTS)))
    return pl.pallas_call(
        functools.partial(_router_kernel, full_f32=full_f32),
        out_shape=jax.ShapeDtypeStruct((m, LANES), F32), grid=(m // tm,),
        in_specs=[pl.BlockSpec((tm, d), lambda i: (i, 0)),
                  pl.BlockSpec((d, LANES), lambda i: (0, 0)),
                  pl.BlockSpec((N_EXPERTS, 1), lambda i: (0, 0))],
        out_specs=pl.BlockSpec((tm, LANES), lambda i: (i, 0)),
        compiler_params=_params("arbitrary"), name="router",
    )(x, rw, router_bias.reshape(N_EXPERTS, 1))


def _moe_dense_kernel(ids_ref, na_ref, x_ref, g_ref, win_ref, wdn_ref, res_ref, lg_ref, lb_ref, *refs,
                      precise):
    out_refs, acc_ref = refs[:-1], refs[-1]
    e = pl.program_id(1)

    @pl.when(e == 0)
    def _():
        acc_ref[...] = jnp.zeros(acc_ref.shape, F32)

    @pl.when(e < na_ref[0])
    def _():
        h = _dot(x_ref[...], win_ref[0], precise)
        hg, hu = h[:, :D_EXPERT], h[:, D_EXPERT:]
        gates = g_ref[...]
        lane = lax.broadcasted_iota(jnp.int32, gates.shape, 1)
        gate = jnp.sum(jnp.where(lane == ids_ref[e], gates, 0.0), axis=1, keepdims=True)
        a = hg * jax.nn.sigmoid(hg) * hu * gate
        acc_ref[...] += _dot(a, wdn_ref[0], precise)

    @pl.when(e == pl.num_programs(1) - 1)
    def _():
        y = _layer_norm_rows(DEEPNORM_ALPHA * res_ref[...] + acc_ref[...], lg_ref[...], lb_ref[...])
        for o in out_refs:
            o[...] = y.astype(o.dtype)


def _moe_dense(x, routed, w_in, w_down, res, ln_g, ln_b, *, tm, precise, out_dtypes):
    m, d = x.shape
    tm = min(tm, m)
    active = jnp.any(routed[:, N_EXPERTS:2 * N_EXPERTS] > 0.5, axis=0)
    order = jnp.argsort(jnp.logical_not(active), stable=True).astype(jnp.int32)
    n_act = jnp.sum(active.astype(jnp.int32))
    ids = jnp.where(jnp.arange(N_EXPERTS) < n_act, order, order[jnp.maximum(n_act - 1, 0)])
    kern = functools.partial(_moe_dense_kernel, precise=precise)
    grid_spec = pltpu.PrefetchScalarGridSpec(
        num_scalar_prefetch=2, grid=(m // tm, N_EXPERTS),
        in_specs=[pl.BlockSpec((tm, d), lambda i, e, ids, na: (i, 0)),
                  pl.BlockSpec((tm, LANES), lambda i, e, ids, na: (i, 0)),
                  pl.BlockSpec((1, d, 2 * D_EXPERT), lambda i, e, ids, na: (ids[e], 0, 0)),
                  pl.BlockSpec((1, D_EXPERT, d), lambda i, e, ids, na: (ids[e], 0, 0)),
                  pl.BlockSpec((tm, d), lambda i, e, ids, na: (i, 0)),
                  pl.BlockSpec((1, d), lambda i, e, ids, na: (0, 0)),
                  pl.BlockSpec((1, d), lambda i, e, ids, na: (0, 0))],
        out_specs=[pl.BlockSpec((tm, d), lambda i, e, ids, na: (i, 0)) for _ in out_dtypes],
        scratch_shapes=[pltpu.VMEM((tm, d), F32)])
    return pl.pallas_call(
        kern, out_shape=[jax.ShapeDtypeStruct((m, d), dt) for dt in out_dtypes], grid_spec=grid_spec,
        compiler_params=_params("arbitrary", "arbitrary"), name="moe_dense",
    )(ids, n_act.reshape(1), x, routed, w_in, w_down, res, ln_g.reshape(1, d), ln_b.reshape(1, d))


def _moe_plan(gates_and_mask, tile):
    gates = gates_and_mask[:, :N_EXPERTS]
    chosen = gates_and_mask[:, N_EXPERTS:2 * N_EXPERTS] > 0.5
    m = gates.shape[0]
    c = chosen.astype(jnp.int32)
    rank = jnp.cumsum(c, axis=0) - c
    counts = jnp.sum(c, axis=0)
    padded = ((counts + tile - 1) // tile) * tile
    ends = jnp.cumsum(padded)
    slot = (ends - padded)[None, :] + rank
    slot_a = jnp.min(jnp.where(chosen, slot, jnp.int32(2 ** 30)), axis=1)
    slot_b = jnp.max(jnp.where(chosen, slot, jnp.int32(-1)), axis=1)
    gate_a = jnp.sum(jnp.where(chosen & (slot == slot_a[:, None]), gates, 0.0), axis=1)
    gate_b = jnp.sum(jnp.where(chosen & (slot == slot_b[:, None]), gates, 0.0), axis=1)
    n_tiles = 2 * m // tile + N_EXPERTS
    starts = jnp.arange(n_tiles, dtype=jnp.int32) * tile
    tile_expert = jnp.minimum(jnp.sum((starts[:, None] >= ends[None, :]).astype(jnp.int32), axis=1),
                              N_EXPERTS - 1)
    n_used = (ends[-1] // tile).astype(jnp.int32).reshape(1)
    return (jnp.stack([slot_a, slot_b], axis=1), jnp.stack([gate_a, gate_b], axis=1),
            tile_expert, n_used, n_tiles, ends.astype(jnp.int32), padded.astype(jnp.int32))


def _row_copy(src, src_row, dst, dst_row, sem):
    return pltpu.make_async_copy(src.at[pl.ds(src_row, 1)], dst.at[pl.ds(dst_row, 1)], sem)


def _moe_dispatch_kernel(ends_ref, padded_ref, slots_ref, x_ref, xs_ref, zero_ref, stage_ref, zero_sem, sems,
                         *, expert_tile):
    i = pl.program_id(0)
    t = slots_ref.shape[2] // 2

    @pl.when(i == 0)
    def _():
        zero_ref[...] = jnp.zeros(zero_ref.shape, zero_ref.dtype)

        def clear_tile(first_row):
            first = pl.multiple_of(first_row, expert_tile)
            return pltpu.make_async_copy(zero_ref, xs_ref.at[pl.ds(first, expert_tile)], zero_sem)

        def clear_all(act):
            for e in range(N_EXPERTS):
                pl.when(padded_ref[e] > 0)(lambda e=e: act(clear_tile(ends_ref[e] - expert_tile)))
            for j in range(xs_ref.shape[0] // expert_tile):
                pl.when(j * expert_tile >= ends_ref[N_EXPERTS - 1])(
                    lambda j=j: act(clear_tile(j * expert_tile)))

        clear_all(lambda cp: cp.start())
        clear_all(lambda cp: cp.wait())

    cur = lax.rem(i, 2)
    stage_ref[cur] = x_ref[...]

    def start(r, carry):
        for k in range(2):
            _row_copy(stage_ref.at[cur], r, xs_ref, slots_ref[0, 0, k * t + r], sems.at[cur]).start(priority=k)
        return carry

    lax.fori_loop(0, t, start, 0, unroll=4)

    def drain(slot):
        rows = xs_ref.at[pl.ds(0, 2 * t)]
        pltpu.make_async_copy(rows, rows, sems.at[slot]).wait()

    pl.when(i > 0)(lambda: drain(1 - cur))
    pl.when(i == pl.num_programs(0) - 1)(lambda: drain(cur))


def _moe_dispatch(x, slots, ends, padded, n_rows, *, tile, expert_tile):
    m, d = x.shape
    nt = m // tile
    slots_t = slots.reshape(nt, tile, 2).transpose(0, 2, 1).reshape(nt, 1, 2 * tile)
    grid_spec = pltpu.PrefetchScalarGridSpec(
        num_scalar_prefetch=2, grid=(nt,),
        in_specs=[pl.BlockSpec((1, 1, 2 * tile), lambda i, en, pa: (i, 0, 0), memory_space=pltpu.SMEM),
                  pl.BlockSpec((tile, d), lambda i, en, pa: (i, 0))],
        out_specs=pl.BlockSpec(memory_space=pl.ANY),
        scratch_shapes=[pltpu.VMEM((expert_tile, d), x.dtype), pltpu.VMEM((2, tile, d), x.dtype),
                        pltpu.SemaphoreType.DMA, pltpu.SemaphoreType.DMA((2,))])
    kern = functools.partial(_moe_dispatch_kernel, expert_tile=expert_tile)
    return pl.pallas_call(kern, out_shape=jax.ShapeDtypeStruct((n_rows, d), x.dtype), grid_spec=grid_spec,
                          compiler_params=_params("arbitrary"), name="moe_dispatch")(ends, padded, slots_t, x)


def _moe_ffn_kernel(te_ref, nu_ref, xs_ref, win_ref, wdn_ref, y_ref, win_bf, wdn_bf):
    i = pl.program_id(0)

    @pl.when(i < nu_ref[0])
    def _():
        prev = te_ref[jnp.maximum(i - 1, 0)]

        @pl.when((i == 0) | (te_ref[i] != prev))
        def _():
            win_bf[...] = win_ref[0].astype(BF16)
            wdn_bf[...] = wdn_ref[0].astype(BF16)

        h = _dot(xs_ref[...], win_bf[...], False)
        hg, hu = h[:, :D_EXPERT], h[:, D_EXPERT:]
        y_ref[...] = _dot(hg * jax.nn.sigmoid(hg) * hu, wdn_bf[...], False)

    @pl.when(i >= nu_ref[0])
    def _():
        y_ref[...] = jnp.zeros(y_ref.shape, F32)


def _moe_ffn(xs, tile_expert, n_used, w_in, w_down, *, tile):
    n_rows, d = xs.shape
    n_tiles = n_rows // tile
    grid_spec = pltpu.PrefetchScalarGridSpec(
        num_scalar_prefetch=2, grid=(n_tiles,),
        in_specs=[pl.BlockSpec((tile, d), lambda i, te, nu: (jnp.minimum(i, nu[0] - 1), 0)),
                  pl.BlockSpec((1, d, 2 * D_EXPERT), lambda i, te, nu: (te[i], 0, 0)),
                  pl.BlockSpec((1, D_EXPERT, d), lambda i, te, nu: (te[i], 0, 0))],
        out_specs=pl.BlockSpec((tile, d), lambda i, te, nu: (i, 0)),
        scratch_shapes=[pltpu.VMEM((d, 2 * D_EXPERT), BF16), pltpu.VMEM((D_EXPERT, d), BF16)])
    return pl.pallas_call(_moe_ffn_kernel, out_shape=jax.ShapeDtypeStruct((n_rows, d), F32),
                          grid_spec=grid_spec, compiler_params=_params("arbitrary"),
                          name="moe_ffn")(tile_expert, n_used, xs, w_in, w_down)


def _moe_combine_kernel(slots_ref, next_slots_ref, y_ref, g_ref, res_ref, lg_ref, lb_ref, *refs):
    out_refs, (ya_ref, yb_ref, sems) = refs[:-3], refs[-3:]
    i = pl.program_id(0)
    t = res_ref.shape[0]
    cur = lax.rem(i, 2)

    def gather(s_ref, buf):
        def start(r, carry):
            _row_copy(y_ref, s_ref[0, 0, r], ya_ref.at[buf], r, sems.at[buf]).start(priority=0)
            _row_copy(y_ref, s_ref[0, 0, t + r], yb_ref.at[buf], r, sems.at[buf]).start(priority=1)
            return carry

        lax.fori_loop(0, t, start, 0, unroll=4)

    pl.when(i == 0)(lambda: gather(slots_ref, 0))
    pl.when(i + 1 < pl.num_programs(0))(lambda: gather(next_slots_ref, 1 - cur))
    for buf_ref in (ya_ref, yb_ref):
        pltpu.make_async_copy(y_ref.at[pl.ds(0, t)], buf_ref.at[cur], sems.at[cur]).wait()
    g = g_ref[...]
    sub = g[:, 0:1] * ya_ref[cur] + g[:, 1:2] * yb_ref[cur]
    y = _layer_norm_rows(DEEPNORM_ALPHA * res_ref[...] + sub, lg_ref[...], lb_ref[...])
    for o in out_refs:
        o[...] = y.astype(o.dtype)


def _moe_combine(y, slots, gates2, res, ln_g, ln_b, *, tile, out_dtypes):
    m, d = res.shape
    nt = m // tile
    slots_t = slots.reshape(nt, tile, 2).transpose(0, 2, 1).reshape(nt, 1, 2 * tile)
    row = pl.BlockSpec((tile, d), lambda i: (i, 0))
    vec = pl.BlockSpec((1, d), lambda i: (0, 0))
    return pl.pallas_call(
        _moe_combine_kernel, out_shape=[jax.ShapeDtypeStruct((m, d), dt) for dt in out_dtypes], grid=(nt,),
        in_specs=[pl.BlockSpec((1, 1, 2 * tile), lambda i: (i, 0, 0), memory_space=pltpu.SMEM),
                  pl.BlockSpec((1, 1, 2 * tile), lambda i: (jnp.minimum(i + 1, nt - 1), 0, 0),
                               memory_space=pltpu.SMEM),
                  pl.BlockSpec(memory_space=pl.ANY),
                  pl.BlockSpec((tile, 2), lambda i: (i, 0)), row, vec, vec],
        out_specs=[row for _ in out_dtypes],
        scratch_shapes=[pltpu.VMEM((2, tile, d), F32), pltpu.VMEM((2, tile, d), F32),
                        pltpu.SemaphoreType.DMA((2,))],
        compiler_params=_params("arbitrary"), name="moe_combine",
    )(slots_t, slots_t, y, gates2, res, ln_g.reshape(1, d), ln_b.reshape(1, d))


def _post_norm_kernel(x_ref, sub_ref, g_ref, b_ref, *out_refs):
    y = _layer_norm_rows(DEEPNORM_ALPHA * x_ref[...] + sub_ref[...], g_ref[...], b_ref[...])
    for o in out_refs:
        o[...] = y.astype(o.dtype)


def _post_norm(x, sub, g, b, *, tm, out_dtypes):
    m, d = x.shape
    tm = min(tm, m)
    row = pl.BlockSpec((tm, d), lambda i: (i, 0))
    vec = pl.BlockSpec((1, d), lambda i: (0, 0))
    return pl.pallas_call(
        _post_norm_kernel, out_shape=[jax.ShapeDtypeStruct((m, d), dt) for dt in out_dtypes],
        grid=(m // tm,), in_specs=[row, row, vec, vec], out_specs=[row for _ in out_dtypes],
        compiler_params=_params("arbitrary"), name="post_norm",
    )(x, sub, g.reshape(1, d), b.reshape(1, d))


PROMPT_TM = 1024
PROMPT_LN_TM = 256
MOE_TM = 512
MOE_ROW_TILE = 256
ATTN_TQ = 512
DSA_TQ = 128
DSA_WIDTH_STEP = 512
DIFF_PAGES_PER_STEP = 8
DSA_PAGES_PER_STEP = 16


def _moe_prompt(xf, xb, router_w, router_bias, w_in, w_down, ln_g, ln_b):
    del xb
    routed = _router(xf, router_w, router_bias, tm=PROMPT_TM, full_f32=False)
    slots, gates2, tile_expert, n_used, n_tiles, ends, padded = _moe_plan(routed, MOE_TM)
    xs = _moe_dispatch(xf, slots, ends, padded, n_tiles * MOE_TM, tile=MOE_ROW_TILE, expert_tile=MOE_TM)
    y = _moe_ffn(xs, tile_expert, n_used, w_in, w_down, tile=MOE_TM)
    return _moe_combine(y, slots, gates2, xf, ln_g, ln_b, tile=MOE_ROW_TILE, out_dtypes=(F32, BF16))


def _moe_sample(xs, router_w, router_bias, w_in, w_down, ln_g, ln_b):
    n = xs.shape[0]
    xpad = jnp.pad(xs, ((0, LANES - n), (0, 0)))
    gates = _router(xpad, router_w, router_bias, tm=LANES, full_f32=True)[:n]
    (y,) = _moe_dense(xs, gates, w_in, w_down, xs, ln_g, ln_b, tm=n, precise=True, out_dtypes=(F32,))
    return y


def _diff_layer(xpf, xpb, xs, cache_k, cache_v, page_table, w_qkv, w_o, lq1, lk1, lq2, lk2, subln_g,
                ln_g, ln_b, layer_idx, b, t):
    lam_init = 0.8 - 0.6 * math.exp(-0.3 * layer_idx)
    lam_vecs = jnp.stack([lq1, lk1, lq2, lk2]).astype(F32)
    dqk = 2 * A_HEADS * A_HEAD_DIM
    cos_p, sin_p = _rope_tables(jnp.arange(t, dtype=jnp.int32), A_HEAD_DIM)
    wb = w_qkv.astype(BF16)
    tm = PROMPT_TM
    (qb,) = _proj(xpb, wb, col_start=0, n_cols=dqk, tm=tm, tn=1024, rope=(cos_p, sin_p, A_HEAD_DIM),
                  out_dtypes=(BF16,), out_scale=A_HEAD_DIM ** -0.5 * LOG2_E, name="diff_q")
    kf, kb = _proj(xpb, wb, col_start=dqk, n_cols=dqk, tm=tm, tn=1024, rope=(cos_p, sin_p, A_HEAD_DIM),
                   out_dtypes=(F32, BF16), name="diff_k")
    vf, vb = _proj(xpb, wb, col_start=2 * dqk, n_cols=dqk, tm=tm, tn=1024, out_dtypes=(F32, BF16),
                   name="diff_v")
    o = _diff_attn_prompt(qb.reshape(b, t, dqk), kb.reshape(b, t, dqk), vb.reshape(b, t, dqk),
                          lam_vecs, subln_g, lam_init, tq=ATTN_TQ)
    x1f, x1b = _proj(o.reshape(b * t, dqk), w_o.astype(BF16), tm=PROMPT_LN_TM, tn=D_MODEL,
                     ln=(xpf, ln_g, ln_b), out_dtypes=(F32, BF16), name="diff_wo")
    bs = xs.shape[0]
    pos_s = jnp.full((bs,), PAST_LEN, jnp.int32)
    cos_s, sin_s = _rope_tables(pos_s, A_HEAD_DIM)
    (qs,) = _proj(xs, w_qkv, col_start=0, n_cols=dqk, tm=bs, tn=1024, precise=True,
                  rope=(cos_s, sin_s, A_HEAD_DIM), name="diff_q_s")
    (ks,) = _proj(xs, w_qkv, col_start=dqk, n_cols=dqk, tm=bs, tn=1024, precise=True,
                  rope=(cos_s, sin_s, A_HEAD_DIM), name="diff_k_s")
    (vs,) = _proj(xs, w_qkv, col_start=2 * dqk, n_cols=dqk, tm=bs, tn=1024, precise=True, name="diff_v_s")
    n_pool = cache_k.shape[0]

    def streams_first(a):
        a = a.reshape(bs, A_HEADS, 2, A_HEAD_DIM)
        return a.transpose(0, 2, 1, 3).reshape(bs, 2 * A_HEADS, A_HEAD_DIM)

    vs_rows = vs.reshape(bs, A_HEADS, 2 * A_HEAD_DIM)
    os_ = _paged_attn(page_table, streams_first(qs),
                      cache_k.reshape(n_pool, PAGE_SIZE * 2 * A_HEADS, A_HEAD_DIM),
                      cache_v.reshape(n_pool, PAGE_SIZE * A_HEADS, 2 * A_HEAD_DIM),
                      streams_first(ks), jnp.concatenate([vs_rows, vs_rows], axis=1),
                      mode="diff", pages_per_step=DIFF_PAGES_PER_STEP, lam_vecs=lam_vecs, subln_g=subln_g,
                      lam_init=lam_init)
    (xs1,) = _proj(os_.reshape(bs, dqk), w_o, tm=bs, tn=D_MODEL, precise=True, ln=(xs, ln_g, ln_b),
                   name="diff_wo_s")
    state = (kf.reshape(b, t, 2 * A_HEADS, A_HEAD_DIM), vf.reshape(b, t, A_HEADS, 2 * A_HEAD_DIM),
             ks.reshape(bs, 1, 2 * A_HEADS, A_HEAD_DIM), vs.reshape(bs, 1, A_HEADS, 2 * A_HEAD_DIM))
    return x1f, x1b, xs1, state


def _dsa_layer(xpf, xpb, xs, cache_k, cache_v, cache_kidx, page_table, w_in, w_o, ln_g, ln_b, b, t):
    dq = B_HEADS * B_HEAD_DIM
    dkv = B_KV_HEADS * B_HEAD_DIM
    dqi = IDX_HEADS * IDX_DIM
    c_k, c_v, c_qi, c_tail = dq, dq + dkv, dq + 2 * dkv, dq + 2 * dkv + dqi
    w_tail = jnp.pad(w_in[:, c_tail:], ((0, 0), (0, LANES - (IDX_DIM + IDX_HEADS))))

    def project(x, w, wt, tm, precise, cos_h, sin_h, cos_i, sin_i, tag):
        both = (F32,) if precise else (F32, BF16)
        low = (F32,) if precise else (BF16,)
        kw = dict(tm=tm, precise=precise)
        q = _proj(x, w, col_start=0, n_cols=dq, tn=1024, rope=(cos_h, sin_h, B_HEAD_DIM), out_dtypes=low,
                  out_scale=1.0 if precise else B_HEAD_DIM ** -0.5 * LOG2_E, name="dsa_q" + tag, **kw)
        k = _proj(x, w, col_start=c_k, n_cols=dkv, tn=dkv, rope=(cos_h, sin_h, B_HEAD_DIM),
                  out_dtypes=both, name="dsa_k" + tag, **kw)
        v = _proj(x, w, col_start=c_v, n_cols=dkv, tn=dkv, out_dtypes=both, name="dsa_v" + tag, **kw)
        qi = _proj(x, w, col_start=c_qi, n_cols=dqi, tn=dqi, rope=(cos_i, sin_i, IDX_DIM),
                   out_dtypes=low, name="dsa_qi" + tag, **kw)
        ki = _proj(x, wt, tn=LANES, rope=(cos_i, sin_i, IDX_DIM), out_dtypes=both,
                   name="dsa_ki" + tag, **kw)
        (wi,) = _proj(x, wt, tn=LANES, name="dsa_wi" + tag, **kw)
        return q, k, v, qi, ki, wi[:, IDX_DIM:IDX_DIM + IDX_HEADS]

    pos_p = jnp.arange(t, dtype=jnp.int32)
    q, k, v, qi, ki, wi = project(xpb, w_in.astype(BF16), w_tail.astype(BF16), PROMPT_TM, False,
                                  *_rope_tables(pos_p, B_HEAD_DIM), *_rope_tables(pos_p, IDX_DIM), "")
    o = _dsa_prompt(qi[0].reshape(b, t, dqi), ki[1][:, :IDX_DIM].reshape(b, t, IDX_DIM),
                    wi.reshape(b, t, IDX_HEADS), q[0].reshape(b, t, dq), k[1].reshape(b, t, dkv),
                    v[1].reshape(b, t, dkv), tq=DSA_TQ)
    x1f, x1b = _proj(o.reshape(b * t, dq), w_o.astype(BF16), tm=PROMPT_LN_TM, tn=D_MODEL,
                     ln=(xpf, ln_g, ln_b), out_dtypes=(F32, BF16), name="dsa_wo")
    bs = xs.shape[0]
    pos_s = jnp.full((bs,), PAST_LEN, jnp.int32)
    qs, ks, vs, qis, kis, wis = project(xs, w_in, w_tail, bs, True, *_rope_tables(pos_s, B_HEAD_DIM),
                                        *_rope_tables(pos_s, IDX_DIM), "_s")
    qs, ks, vs, qis, kis = qs[0], ks[0], vs[0], qis[0], kis[0][:, :IDX_DIM]
    qis3 = qis.reshape(bs, IDX_HEADS, IDX_DIM)
    wis3 = wis.reshape(bs, IDX_HEADS, 1)
    n_pages = page_table.shape[1]
    past_scores = _idx_scores_paged(page_table, qis3, wis3, jnp.swapaxes(cache_kidx, 1, 2),
                                    pages_per_step=DSA_PAGES_PER_STEP)
    new_page = jnp.pad(kis.reshape(bs, IDX_DIM, 1), ((0, 0), (0, 0), (0, PAGE_SIZE - 1)))
    new_scores = _idx_scores_paged(jnp.arange(bs, dtype=jnp.int32).reshape(bs, 1), qis3, wis3, new_page,
                                   pages_per_step=1)
    n_past = n_pages * PAGE_SIZE
    all_scores = jnp.concatenate(
        [past_scores.reshape(bs, n_past), new_scores[:, 0, :1], jnp.full((bs, LANES - 1), -jnp.inf, F32)], axis=1)
    bias_all = _topk_bias(all_scores, min(IDX_TOPK_MAX, (n_past + 1) // 4))
    n_pool = cache_k.shape[0]
    per_head = lambda a: jnp.repeat(a.reshape(bs, B_KV_HEADS, B_HEAD_DIM), B_GROUP, axis=1)
    os_ = _paged_attn(page_table, qs.reshape(bs, B_HEADS, B_HEAD_DIM),
                      cache_k.reshape(n_pool, PAGE_SIZE * B_KV_HEADS, B_HEAD_DIM),
                      cache_v.reshape(n_pool, PAGE_SIZE * B_KV_HEADS, B_HEAD_DIM),
                      per_head(ks), per_head(vs), mode="dsa", pages_per_step=DSA_PAGES_PER_STEP,
                      bias=jnp.repeat(bias_all[:, :n_past], B_KV_HEADS, axis=1).reshape(
                          bs, n_pages, PAGE_SIZE * B_KV_HEADS),
                      bias_new=bias_all[:, n_past:n_past + 1].reshape(bs, 1, 1))
    (xs1,) = _proj(os_.reshape(bs, dq), w_o, tm=bs, tn=D_MODEL, precise=True, ln=(xs, ln_g, ln_b),
                   name="dsa_wo_s")
    state = (k[0].reshape(b, t, B_KV_HEADS, B_HEAD_DIM), v[0].reshape(b, t, B_KV_HEADS, B_HEAD_DIM),
             ki[0][:, :IDX_DIM].reshape(b, t, IDX_DIM),
             ks.reshape(bs, 1, B_KV_HEADS, B_HEAD_DIM), vs.reshape(bs, 1, B_KV_HEADS, B_HEAD_DIM),
             kis.reshape(bs, 1, IDX_DIM))
    return x1f, x1b, xs1, state


def _pool_layer(xpf, xs, state_pool, w_group, scale, ln_g, ln_b, b, t):
    xp3 = xpf.reshape(b, t, D_MODEL)
    sub = _pool_prompt(xp3, w_group.astype(BF16), scale)
    x1f, x1b = _post_norm(xpf, sub.reshape(b * t, D_MODEL), ln_g, ln_b, tm=PROMPT_LN_TM,
                          out_dtypes=(F32, BF16))
    bs = xs.shape[0]
    xs_ext = jnp.concatenate([state_pool.astype(xs.dtype), xs.reshape(bs, 1, D_MODEL)], axis=1)
    sub_s = _pool_sample(xs_ext, w_group, scale)
    (xs1,) = _post_norm(xs, sub_s, ln_g, ln_b, tm=bs, out_dtypes=(F32,))
    return x1f, x1b, xs1, (xp3[:, -POOL_STATE_LEN:], xs_ext[:, -POOL_STATE_LEN:])


def kernel(x_prompt, x_sample, cache_l0_k, cache_l0_v, cache_l1_k, cache_l1_v, cache_l1_kidx, state_l2_pool, cache_l3_k, cache_l3_v, page_table, router_w, router_bias, l0_w_qkv, l0_w_o, l0_lam_q1, l0_lam_k1, l0_lam_q2, l0_lam_k2, l0_subln_g, l0_ln1_g, l0_ln1_b, l0_moe_w_in, l0_moe_w_down, l0_ln2_g, l0_ln2_b, l1_w_in, l1_w_o, l1_ln1_g, l1_ln1_b, l1_moe_w_in, l1_moe_w_down, l1_ln2_g, l1_ln2_b, l2_w_group, l2_scale, l2_ln1_g, l2_ln1_b, l2_moe_w_in, l2_moe_w_down, l2_ln2_g, l2_ln2_b, l3_w_qkv, l3_w_o, l3_lam_q1, l3_lam_k1, l3_lam_q2, l3_lam_k2, l3_subln_g, l3_ln1_g, l3_ln1_b, l3_moe_w_in, l3_moe_w_down, l3_ln2_g, l3_ln2_b):
    b, t, d = x_prompt.shape
    bs = x_sample.shape[0]
    xpf = x_prompt.reshape(b * t, d)
    xpb = xpf.astype(BF16)
    xs = x_sample.reshape(bs, d)
    moe_p = [(l0_moe_w_in, l0_moe_w_down, l0_ln2_g, l0_ln2_b), (l1_moe_w_in, l1_moe_w_down, l1_ln2_g, l1_ln2_b),
             (l2_moe_w_in, l2_moe_w_down, l2_ln2_g, l2_ln2_b), (l3_moe_w_in, l3_moe_w_down, l3_ln2_g, l3_ln2_b)]
    states = []
    for i in range(DEPTH):
        if i == 0:
            xpf, xpb, xs, st = _diff_layer(xpf, xpb, xs, cache_l0_k, cache_l0_v, page_table, l0_w_qkv, l0_w_o,
                                           l0_lam_q1, l0_lam_k1, l0_lam_q2, l0_lam_k2, l0_subln_g,
                                           l0_ln1_g, l0_ln1_b, i, b, t)
        elif i == 1:
            xpf, xpb, xs, st = _dsa_layer(xpf, xpb, xs, cache_l1_k, cache_l1_v, cache_l1_kidx, page_table,
                                          l1_w_in, l1_w_o, l1_ln1_g, l1_ln1_b, b, t)
        elif i == 2:
            xpf, xpb, xs, st = _pool_layer(xpf, xs, state_l2_pool, l2_w_group, l2_scale, l2_ln1_g, l2_ln1_b, b, t)
        else:
            xpf, xpb, xs, st = _diff_layer(xpf, xpb, xs, cache_l3_k, cache_l3_v, page_table, l3_w_qkv, l3_w_o,
                                           l3_lam_q1, l3_lam_k1, l3_lam_q2, l3_lam_k2, l3_subln_g,
                                           l3_ln1_g, l3_ln1_b, i, b, t)
        states.append(st)
        w_in, w_down, ln_g, ln_b = moe_p[i]
        xpf, xpb = _moe_prompt(xpf, xpb, router_w, router_bias, w_in, w_down, ln_g, ln_b)
        xs = _moe_sample(xs, router_w, router_bias, w_in, w_down, ln_g, ln_b)
    l0, l1, l2, l3 = states
    return (xpf.reshape(b, t, d), xs.reshape(bs, 1, d),
            l0[0], l0[1], l0[2], l0[3],
            l1[0], l1[1], l1[2], l1[3], l1[4], l1[5],
            l2[0], l2[1],
            l3[0], l3[1], l3[2], l3[3])
```
